```python
import math
import jax
import jax.numpy as jnp
from jax import lax
import numpy as np

D_MODEL = 1024
BATCH = 4
SEQ = 8192
DEPTH = 4

HEAD_DIM = 64
MIX_HALF = D_MODEL // 2
A_Q_HEADS = MIX_HALF // HEAD_DIM
A_KV_HEADS = A_Q_HEADS // 4
A_WINDOW = 128
B_KEY_DIM = 128
B_HEADS = MIX_HALF // B_KEY_DIM
B_VAL_DIM = MIX_HALF // B_HEADS
B_CHUNK = 64
B_MIN_F = 1e-30
C_VAL_DIM = 128
C_HEADS = MIX_HALF // C_VAL_DIM
C_KEY_DIM = C_VAL_DIM // 2
C_CHUNK = 128
RET_THETA = 10000.0
D_Q_HEADS = MIX_HALF // HEAD_DIM
D_KV_HEADS = D_Q_HEADS // 4
D_PATTERNS = ((128, 1), (512, 4), (2048, 16))
ATTN_BLOCK = 128
ROPE_THETA = 500000.0
ROPE_DIM = HEAD_DIM // 4
N_EXPERTS = 32
TOP_K = 4
D_EXPERT = D_MODEL
SWIGLU_LIMIT = 7.0
SWIGLU_ALPHA = 1.702
MOE_BLOCK = 512
DN_ALPHA = (2 * DEPTH) ** 0.25
DN_BETA = (8 * DEPTH) ** -0.25
LN_EPS = 1e-5
NORM_EPS = 1e-6
N_EVEN = (DEPTH + 1) // 2
N_ODD = DEPTH // 2

EVEN_WIDTHS = (A_Q_HEADS * HEAD_DIM, A_KV_HEADS * HEAD_DIM, A_KV_HEADS * HEAD_DIM,
               B_HEADS * B_KEY_DIM, B_HEADS * B_KEY_DIM, B_HEADS * B_VAL_DIM, B_HEADS * B_VAL_DIM)
ODD_WIDTHS = (C_HEADS * C_KEY_DIM, C_HEADS * C_KEY_DIM, C_HEADS * C_VAL_DIM, C_HEADS * C_VAL_DIM) + \
    (D_Q_HEADS * HEAD_DIM, D_KV_HEADS * HEAD_DIM, D_KV_HEADS * HEAD_DIM) * len(D_PATTERNS)
EVEN_COLS = sum(EVEN_WIDTHS)
ODD_COLS = sum(ODD_WIDTHS)
EVEN_MIX = A_Q_HEADS * HEAD_DIM + B_HEADS * B_VAL_DIM
ODD_MIX = C_HEADS * C_VAL_DIM + D_Q_HEADS * HEAD_DIM

kernel_name = 'hybrid_swa_hgrn2_retention_dilated_moe_deepnorm'


def split_cols(h, widths):
    offs, acc = [], 0
    for w in widths[:-1]:
        acc += w
        offs.append(acc)
    return jnp.split(h, offs, axis=-1)


def layer_norm(x, g, b):
    xf = x.astype(jnp.float32)
    mu = jnp.mean(xf, -1, keepdims=True)
    var = jnp.mean(jnp.square(xf - mu), -1, keepdims=True)
    return ((xf - mu) * lax.rsqrt(var + LN_EPS) * g + b).astype(x.dtype)


def head_norm(o, gain, center):
    B, T = o.shape[:2]
    if center:
        o = o - jnp.mean(o, -1, keepdims=True)
    o = o * lax.rsqrt(jnp.mean(o * o, -1, keepdims=True) + NORM_EPS)
    return o.reshape(B, T, -1) * gain.astype(jnp.float32)


def rotary_tables(pos, inv_freq):
    ang = pos[:, None] * inv_freq[None, :]
    return jnp.cos(ang), jnp.sin(ang)


def apply_rotary(x, rope):
    cos, sin = rope
    half = cos.shape[-1]
    c = cos[None, :, None, :].astype(x.dtype)
    s = sin[None, :, None, :].astype(x.dtype)
    x1, x2, rest = x[..., :half], x[..., half:2 * half], x[..., 2 * half:]
    return jnp.concatenate([x1 * c - x2 * s, x2 * c + x1 * s, rest], axis=-1)


def banded_attention(q, k, v, max_dist, sink=None):
    N, Hkv, G, L, hd = q.shape
    nb = L // ATTN_BLOCK
    qb = q.reshape(N, Hkv, G, nb, ATTN_BLOCK, hd).astype(jnp.float32)

    def with_prev(t):
        tb = t.reshape(N, Hkv, nb, ATTN_BLOCK, hd).astype(jnp.float32)
        prev = jnp.concatenate([jnp.zeros_like(tb[:, :, :1]), tb[:, :, :-1]], axis=2)
        return jnp.concatenate([prev, tb], axis=3)

    kb, vb = with_prev(k), with_prev(v)
    s = jnp.einsum('nhgbqd,nhbkd->nhgbqk', qb, kb) * (hd ** -0.5)
    qi = jnp.arange(ATTN_BLOCK)[:, None]
    kj = jnp.arange(2 * ATTN_BLOCK)[None, :]
    dist = qi - kj + ATTN_BLOCK
    blk = jnp.arange(nb)[:, None, None]
    valid = (dist >= 0) & (dist <= max_dist) & ((blk > 0) | (kj >= ATTN_BLOCK))
    s = jnp.where(valid, s, -jnp.inf)
    if sink is not None:
        sk = jnp.broadcast_to(sink.astype(jnp.float32)[None, :, :, None, None, None], s.shape[:-1] + (1,))
        s = jnp.concatenate([s, sk], axis=-1)
    lse = jax.nn.logsumexp(s, axis=-1)
    p = jnp.exp(s - lse[..., None])
    if sink is not None:
        p = p[..., :-1]
    o = jnp.einsum('nhgbqk,nhbkd->nhgbqd', p, vb)
    return o.reshape(N, Hkv, G, L, hd).astype(v.dtype), lse.reshape(N, Hkv, G, L)


def swa_sink_attention(q, k, v, sinks, rope):
    B, T, Hq, hd = q.shape
    Hkv = k.shape[2]
    G = Hq // Hkv
    q = apply_rotary(q, rope)
    k = apply_rotary(k, rope)
    qh = q.reshape(B, T, Hkv, G, hd).transpose(0, 2, 3, 1, 4)
    o, _ = banded_attention(qh, k.transpose(0, 2, 1, 3), v.transpose(0, 2, 1, 3),
                            A_WINDOW - 1, sinks.reshape(Hkv, G))
    return o.transpose(0, 3, 1, 2, 4).reshape(B, T, Hq * hd)


def hgrn2(q, f_logit, i_in, g, lb, norm_g):
    B, T, H, dk = q.shape
    dv = i_in.shape[-1]
    lb = jnp.clip(lb.reshape(H, dk).astype(jnp.float32), 0.0, 1.0)
    f = lb + (1.0 - lb) * jax.nn.sigmoid(f_logit.astype(jnp.float32))
    log_f = jnp.log(jnp.maximum(f, B_MIN_F))
    key = 1.0 - f
    C = B_CHUNK
    nc = T // C

    def chunks(t):
        return t.astype(jnp.float32).reshape(B, nc, C, H, t.shape[-1]).transpose(1, 0, 3, 2, 4)

    causal = jnp.tril(jnp.ones((C, C), dtype=bool))[:, :, None]

    def step(S, xs):
        qc, kc, lfc, vc = xs
        b = jnp.cumsum(lfc, axis=2)
        pair = jnp.exp(jnp.where(causal, b[:, :, :, None, :] - b[:, :, None, :, :], -jnp.inf))
        att = jnp.einsum('bhtc,bhtsc,bhsc->bhts', qc, pair, kc)
        o = jnp.einsum('bhts,bhsv->bhtv', att, vc) + jnp.einsum('bhtc,bhcv->bhtv', qc * jnp.exp(b), S)
        b_last = b[:, :, -1:, :]
        S = jnp.exp(b_last[:, :, 0, :])[..., None] * S + \
            jnp.einsum('bhsc,bhsv->bhcv', kc * jnp.exp(b_last - b), vc)
        return S, o

    S0 = jnp.zeros((B, H, dk, dv), jnp.float32)
    _, o = lax.scan(step, S0, (chunks(q), chunks(key), chunks(log_f), chunks(i_in)))
    o = o.transpose(1, 0, 3, 2, 4).reshape(B, T, H, dv)
    y = head_norm(o, norm_g, center=False) * jax.nn.silu(g.astype(jnp.float32))
    return y.astype(i_in.dtype)


def retention(q, k, v, g, norm_g, rope):
    B, T, H, dk = q.shape
    dv = v.shape[-1]
    out_dtype = v.dtype
    q = apply_rotary(q, rope).astype(jnp.float32)
    k = apply_rotary(k, rope).astype(jnp.float32) * (dk ** -0.5)
    v = v.astype(jnp.float32)
    C = C_CHUNK
    nc = T // C
    log_gamma = jnp.log1p(-jnp.exp2(-5.0 - jnp.arange(H, dtype=jnp.float32)))

    def chunks(t):
        return t.reshape(B, nc, C, H, t.shape[-1]).transpose(0, 3, 1, 2, 4)

    qc, kc, vc = chunks(q), chunks(k), chunks(v)
    i = jnp.arange(C, dtype=jnp.float32)
    rel = i[:, None] - i[None, :]
    decay = jnp.where(rel >= 0, jnp.exp(log_gamma[:, None, None] * jnp.maximum(rel, 0.0)), 0.0)
    scores = jnp.einsum('bhnid,bhnjd->bhnij', qc, kc) * decay[None, :, None]
    o_intra = jnp.einsum('bhnij,bhnjv->bhniv', scores, vc)
    k_tail = kc * jnp.exp(log_gamma[:, None] * (C - 1 - i)[None, :])[None, :, None, :, None]
    R = jnp.einsum('bhnjd,bhnjv->bhndv', k_tail, vc)
    chunk_decay = jnp.exp(log_gamma * C)[None, :, None, None]

    def step(S, Rn):
        return chunk_decay * S + Rn, S

    _, S_prev = lax.scan(step, jnp.zeros((B, H, dk, dv), jnp.float32), R.transpose(2, 0, 1, 3, 4))
    S_prev = S_prev.transpose(1, 2, 0, 3, 4)
    q_head = qc * jnp.exp(log_gamma[:, None] * (i + 1.0)[None, :])[None, :, None, :, None]
    o_inter = jnp.einsum('bhnid,bhndv->bhniv', q_head, S_prev)
    o = (o_intra + o_inter).transpose(0, 2, 3, 1, 4).reshape(B, T, H, dv)
    y = head_norm(o, norm_g, center=True) * jax.nn.silu(g.astype(jnp.float32))
    return y.astype(out_dtype)


def dilated_attention(qs, ks, vs, rope):
    outs, lses = [], []
    for (window, dil), q, k, v in zip(D_PATTERNS, qs, ks, vs):
        B, T, Hq, hd = q.shape
        Hkv = k.shape[2]
        G = Hq // Hkv
        q = apply_rotary(q, rope)
        k = apply_rotary(k, rope)
        span = dil * ATTN_BLOCK
        Tp = -(-T // span) * span
        L = Tp // dil
        pad = ((0, 0), (0, Tp - T), (0, 0), (0, 0))
        qg = jnp.pad(q, pad).reshape(B, L, dil, Hkv, G, hd).transpose(0, 2, 3, 4, 1, 5).reshape(B * dil, Hkv, G, L, hd)
        kg = jnp.pad(k, pad).reshape(B, L, dil, Hkv, hd).transpose(0, 2, 3, 1, 4).reshape(B * dil, Hkv, L, hd)
        vg = jnp.pad(v, pad).reshape(B, L, dil, Hkv, hd).transpose(0, 2, 3, 1, 4).reshape(B * dil, Hkv, L, hd)
        o, lse = banded_attention(qg, kg, vg, window // dil)
        o = o.reshape(B, dil, Hkv, G, L, hd).transpose(0, 4, 1, 2, 3, 5).reshape(B, Tp, Hq, hd)[:, :T]
        lse = lse.reshape(B, dil, Hkv, G, L).transpose(0, 4, 1, 2, 3).reshape(B, Tp, Hq)[:, :T]
        outs.append(o.astype(jnp.float32))
        lses.append(lse)
    w = jax.nn.softmax(jnp.stack(lses, axis=0), axis=0)
    o = jnp.sum(w[..., None] * jnp.stack(outs, axis=0), axis=0)
    B, T, Hq, hd = o.shape
    return o.reshape(B, T, Hq * hd).astype(vs[0].dtype)


def even_mixer(x, w_in, b_in, sinks, lb, norm_g, w_out, rope):
    B, T, _ = x.shape
    h = x @ w_in + b_in
    aq, ak, av, bq, bf, bi, bg = split_cols(h, EVEN_WIDTHS)
    ya = swa_sink_attention(aq.reshape(B, T, A_Q_HEADS, HEAD_DIM),
                            ak.reshape(B, T, A_KV_HEADS, HEAD_DIM),
                            av.reshape(B, T, A_KV_HEADS, HEAD_DIM), sinks, rope)
    yb = hgrn2(bq.reshape(B, T, B_HEADS, B_KEY_DIM), bf.reshape(B, T, B_HEADS, B_KEY_DIM),
               bi.reshape(B, T, B_HEADS, B_VAL_DIM), bg, lb, norm_g)
    return jnp.concatenate([ya, yb], axis=-1) @ w_out


def odd_mixer(x, w_in, b_in, norm_g, w_out, rope, ret_rope):
    B, T, _ = x.shape
    h = x @ w_in + b_in
    parts = split_cols(h, ODD_WIDTHS)
    cq, ck, cv, cg = parts[:4]
    yc = retention(cq.reshape(B, T, C_HEADS, C_KEY_DIM), ck.reshape(B, T, C_HEADS, C_KEY_DIM),
                   cv.reshape(B, T, C_HEADS, C_VAL_DIM), cg, norm_g, ret_rope)
    dparts = parts[4:]
    qs = [dparts[3 * p].reshape(B, T, D_Q_HEADS, HEAD_DIM) for p in range(len(D_PATTERNS))]
    ks = [dparts[3 * p + 1].reshape(B, T, D_KV_HEADS, HEAD_DIM) for p in range(len(D_PATTERNS))]
    vs = [dparts[3 * p + 2].reshape(B, T, D_KV_HEADS, HEAD_DIM) for p in range(len(D_PATTERNS))]
    yd = dilated_attention(qs, ks, vs, rope)
    return jnp.concatenate([yc, yd], axis=-1) @ w_out


def moe_ffn(h, w_router, b_router, w_gu, b_gu, w_dn, b_dn):
    B, T, D = h.shape
    N = B * T
    NK = N * TOP_K
    hf = h.reshape(N, D)
    logits = (hf @ w_router).astype(jnp.float32) + b_router.astype(jnp.float32)
    top_val, top_idx = lax.top_k(logits, TOP_K)
    gates = jax.nn.softmax(top_val, axis=-1)
    flat_e = top_idx.reshape(NK)
    flat_tok = jnp.arange(NK, dtype=jnp.int32) // TOP_K
    order = jnp.argsort(flat_e)
    sorted_e = flat_e[order]
    counts = jnp.bincount(flat_e, length=N_EXPERTS)
    padded = (counts + MOE_BLOCK - 1) // MOE_BLOCK * MOE_BLOCK
    pad_end = jnp.cumsum(padded)
    pad_start = pad_end - padded
    start = jnp.cumsum(counts) - counts
    dest = pad_start[sorted_e] + jnp.arange(NK, dtype=jnp.int32) - start[sorted_e]
    n_blocks = NK // MOE_BLOCK + N_EXPERTS + 1
    cap = n_blocks * MOE_BLOCK
    rows = jnp.full((cap,), N, dtype=jnp.int32).at[dest].set(flat_tok[order])
    gate_rows = jnp.zeros((cap,), jnp.float32).at[dest].set(gates.reshape(NK)[order])
    block_e = jnp.minimum(jnp.searchsorted(pad_end, jnp.arange(n_blocks, dtype=jnp.int32) * MOE_BLOCK,
                                           side='right'), N_EXPERTS - 1)
    h_pad = jnp.concatenate([hf, jnp.zeros((1, D), hf.dtype)], axis=0)

    def expert_block(args):
        rb, gb, e = args
        xb = h_pad[rb]
        gu = xb @ w_gu[e] + b_gu[e]
        gate = jnp.minimum(gu[:, :D_EXPERT], SWIGLU_LIMIT)
        up = jnp.clip(gu[:, D_EXPERT:], -SWIGLU_LIMIT, SWIGLU_LIMIT)
        act = (up + 1.0) * gate * jax.nn.sigmoid(SWIGLU_ALPHA * gate)
        y = act @ w_dn[e] + b_dn[e]
        return y * gb[:, None].astype(y.dtype)

    y = lax.map(expert_block, (rows.reshape(n_blocks, MOE_BLOCK), gate_rows.reshape(n_blocks, MOE_BLOCK), block_e))
    out = jnp.zeros((N + 1, D), y.dtype).at[rows].add(y.reshape(cap, D))
    return out[:N].reshape(B, T, D).astype(h.dtype)


def setup_inputs(seed: int = 0) -> dict:
    key = jax.random.key(seed)
    ks = jax.random.split(key, 24)

    def nrm(k, shape, scale):
        return jax.random.normal(k, shape, jnp.float32) * scale

    return {
        'x': nrm(ks[0], (BATCH, SEQ, D_MODEL), 1.0),
        'w_in_even': nrm(ks[1], (N_EVEN, D_MODEL, EVEN_COLS), D_MODEL ** -0.5),
        'b_in_even': nrm(ks[2], (N_EVEN, EVEN_COLS), 0.02),
        'attn_sinks': nrm(ks[3], (N_EVEN, A_Q_HEADS), 0.5),
        'hgrn_lb_logits': nrm(ks[4], (N_EVEN, B_HEADS * B_KEY_DIM), 1.0),
        'hgrn_norm': 1.0 + nrm(ks[5], (N_EVEN, B_HEADS * B_VAL_DIM), 0.02),
        'w_out_even': nrm(ks[6], (N_EVEN, EVEN_MIX, D_MODEL), EVEN_MIX ** -0.5 * DN_BETA),
        'w_in_odd': nrm(ks[7], (N_ODD, D_MODEL, ODD_COLS), D_MODEL ** -0.5),
        'b_in_odd': nrm(ks[8], (N_ODD, ODD_COLS), 0.02),
        'ret_norm': 1.0 + nrm(ks[9], (N_ODD, C_HEADS * C_VAL_DIM), 0.02),
        'w_out_odd': nrm(ks[10], (N_ODD, ODD_MIX, D_MODEL), ODD_MIX ** -0.5 * DN_BETA),
        'ln1_g': 1.0 + nrm(ks[11], (DEPTH, D_MODEL), 0.02),
        'ln1_b': nrm(ks[12], (DEPTH, D_MODEL), 0.02),
        'ln2_g': 1.0 + nrm(ks[13], (DEPTH, D_MODEL), 0.02),
        'ln2_b': nrm(ks[14], (DEPTH, D_MODEL), 0.02),
        'router_w': nrm(ks[15], (DEPTH, D_MODEL, N_EXPERTS), D_MODEL ** -0.5),
        'router_b': nrm(ks[16], (DEPTH, N_EXPERTS), 0.01),
        'expert_w_gu': nrm(ks[17], (DEPTH, N_EXPERTS, D_MODEL, 2 * D_EXPERT), D_MODEL ** -0.5),
        'expert_b_gu': nrm(ks[18], (DEPTH, N_EXPERTS, 2 * D_EXPERT), 0.02),
        'expert_w_dn': nrm(ks[19], (DEPTH, N_EXPERTS, D_EXPERT, D_MODEL), D_EXPERT ** -0.5 * DN_BETA),
        'expert_b_dn': nrm(ks[20], (DEPTH, N_EXPERTS, D_MODEL), 0.02),
    }


def reference(x, w_in_even, b_in_even, attn_sinks, hgrn_lb_logits, hgrn_norm, w_out_even,
              w_in_odd, b_in_odd, ret_norm, w_out_odd, ln1_g, ln1_b, ln2_g, ln2_b,
              router_w, router_b, expert_w_gu, expert_b_gu, expert_w_dn, expert_b_dn):
    T = x.shape[1]
    pos = jnp.arange(T, dtype=jnp.float32)
    rope_inv = 1.0 / (ROPE_THETA ** (jnp.arange(0, ROPE_DIM, 2, dtype=jnp.float32) / ROPE_DIM))
    ret_inv = 1.0 / (RET_THETA ** jnp.linspace(0.0, 1.0, C_KEY_DIM // 2, dtype=jnp.float32))
    rope = rotary_tables(pos, rope_inv)
    ret_rope = rotary_tables(pos, ret_inv)
    lb_soft = jax.nn.softmax(hgrn_lb_logits.astype(jnp.float32), axis=0)
    lower_bounds = jnp.concatenate([jnp.zeros_like(lb_soft[:1]), jnp.cumsum(lb_soft, axis=0)[:-1]], axis=0)
    for layer in range(DEPTH):
        j = layer // 2
        if layer % 2 == 0:
            mix = even_mixer(x, w_in_even[j], b_in_even[j], attn_sinks[j], lower_bounds[j],
                             hgrn_norm[j], w_out_even[j], rope)
        else:
            mix = odd_mixer(x, w_in_odd[j], b_in_odd[j], ret_norm[j], w_out_odd[j], rope, ret_rope)
        x = layer_norm(DN_ALPHA * x + mix, ln1_g[layer], ln1_b[layer])
        ffn = moe_ffn(x, router_w[layer], router_b[layer], expert_w_gu[layer], expert_b_gu[layer],
                      expert_w_dn[layer], expert_b_dn[layer])
        x = layer_norm(DN_ALPHA * x + ffn, ln2_g[layer], ln2_b[layer])
    return x
```

```python
import functools
import math

import jax
import jax.numpy as jnp
from jax import lax
from jax.experimental import pallas as pl
from jax.experimental.pallas import tpu as pltpu

F32 = jnp.float32
BF16 = jnp.bfloat16

HEAD_DIM = 64
ATTN_BLOCK = 128
A_Q_HEADS = 8
A_KV_HEADS = 2
A_WINDOW = 128
B_HEADS = 4
B_KEY_DIM = 128
B_MIN_F = 1e-30
C_HEADS = 4
C_KEY_DIM = 64
C_VAL_DIM = 128
D_PATTERNS = ((128, 1), (512, 4), (2048, 16))
ROPE_THETA = 500000.0
ROPE_DIM = HEAD_DIM // 4
RET_THETA = 10000.0
N_EXPERTS = 32
TOP_K = 4
SWIGLU_LIMIT = 7.0
SWIGLU_ALPHA = 1.702
MOE_BLOCK = 512
LN_EPS = 1e-5
NORM_EPS = 1e-6
NEG_BIG = -1e30

LANES = 128
VMEM_LIMIT = 48 * 1024 * 1024

HGRN_CHUNK = 128
HGRN_SUB = 8
HGRN_LEVELS = (16, 32, 64, 128)


def _cparams(*sem):
    return pltpu.CompilerParams(dimension_semantics=sem, vmem_limit_bytes=VMEM_LIMIT)


def _sigmoid(z):
    return 1.0 / (1.0 + jnp.exp(-z))


def _rotate(h, cos, sa, sb, shift):
    width = h.shape[1]
    rep = width // LANES
    if rep > 1:
        cos = jnp.concatenate([cos] * rep, axis=1)
        sa = jnp.concatenate([sa] * rep, axis=1)
        sb = jnp.concatenate([sb] * rep, axis=1)
    up = pltpu.roll(h, width - shift, axis=1)
    dn = pltpu.roll(h, shift, axis=1)
    return h * cos + up * sa + dn * sb


def _inproj_kernel(x_ref, w_ref, b_ref, ta_ref, tr_ref, *out_refs, plan):
    xb = x_ref[...].astype(BF16)
    off = 0
    for o_ref, (width, rope, scale) in zip(out_refs, plan):
        h = jnp.dot(xb, w_ref[:, off:off + width], preferred_element_type=F32)
        h = h + b_ref[:, off:off + width]
        if rope == "attn":
            h = _rotate(h, ta_ref[0], ta_ref[1], ta_ref[2], ROPE_DIM // 2)
        elif rope == "ret":
            h = _rotate(h, tr_ref[0], tr_ref[1], tr_ref[2], C_KEY_DIM // 2)
        if scale != 1.0:
            h = h * scale
        o_ref[...] = h.astype(o_ref.dtype)
        off += width


def _inproj(x, w, b, tab_attn, tab_ret, plan, seq, tm=256):
    n, d = x.shape
    cols = w.shape[1]
    tpb = seq // tm
    kern = functools.partial(_inproj_kernel, plan=plan)
    outs = tuple(jax.ShapeDtypeStruct((n, wd), F32) for wd, _, _ in plan)
    return pl.pallas_call(
        kern,
        out_shape=outs,
        grid=(n // tm,),
        in_specs=[
            pl.BlockSpec((tm, d), lambda i: (i, 0)),
            pl.BlockSpec((d, cols), lambda i: (0, 0)),
            pl.BlockSpec((1, cols), lambda i: (0, 0)),
            pl.BlockSpec((3, tm, LANES), lambda i: (0, i % tpb, 0)),
            pl.BlockSpec((3, tm, LANES), lambda i: (0, i % tpb, 0)),
        ],
        out_specs=tuple(pl.BlockSpec((tm, wd), lambda i: (i, 0)) for wd, _, _ in plan),
        compiler_params=_cparams("parallel"),
        name="inproj",
    )(x, w, b, tab_attn, tab_ret)


def _rope_tables(seq, half, inv_freq, period):
    pos = jnp.arange(seq, dtype=F32)
    ang = pos[:, None] * inv_freq[None, :]
    cos, sin = jnp.cos(ang), jnp.sin(ang)
    pad = period - 2 * half
    ones = jnp.ones((seq, pad), F32)
    zeros = jnp.zeros((seq, pad), F32)
    zh = jnp.zeros((seq, half), F32)
    cosf = jnp.concatenate([cos, cos, ones], axis=1)
    sa = jnp.concatenate([-sin, zh, zeros], axis=1)
    sb = jnp.concatenate([zh, sin, zeros], axis=1)
    rep = LANES // period
    return jnp.stack([jnp.tile(t, (1, rep)) for t in (cosf, sa, sb)], axis=0)


def _attn_kernel(*refs, dil, max_dist, nsub, has_sink, want_lse):
    if has_sink:
        sink_ref, refs = refs[0], refs[1:]
    q_refs = refs[:4]
    k_ref, v_ref, kp_ref, vp_ref = refs[4:8]
    o_ref = refs[8]
    lse_ref = refs[9] if want_lse else None
    n_out = 2 if want_lse else 1
    stage = refs[8 + n_out:]
    blk = ATTN_BLOCK
    qi = lax.broadcasted_iota(jnp.int32, (4 * blk, 2 * blk), 0) % blk
    kj = lax.broadcasted_iota(jnp.int32, (4 * blk, 2 * blk), 1)
    dist = qi - kj + blk
    band = (dist >= 0) & (dist <= max_dist)
    first_kmin = jnp.where(pl.program_id(1) == 0, blk, 0)
    band_first = band & (kj >= first_kmin)
    row4 = lax.broadcasted_iota(jnp.int32, (4 * blk, 1), 0) // blk

    def take(start):
        return pl.ds(start, blk, stride=dil) if dil > 1 else pl.ds(start, blk)

    def one_group(r, j):
        rows = take(j * blk * dil + r)
        if j == 0:
            k_prev, v_prev = kp_ref[take(r), :], vp_ref[take(r), :]
            valid = band_first
        else:
            prow = take((j - 1) * blk * dil + r)
            k_prev, v_prev = k_ref[prow, :], v_ref[prow, :]
            valid = band
        k_cat = jnp.concatenate([k_prev, k_ref[rows, :]], axis=0).astype(BF16)
        v_cat = jnp.concatenate([v_prev, v_ref[rows, :]], axis=0).astype(BF16)
        for h in range(A_KV_HEADS):
            qa, qb = q_refs[2 * h][rows, :], q_refs[2 * h + 1][rows, :]
            q4 = jnp.concatenate([qa[:, :HEAD_DIM], qa[:, HEAD_DIM:], qb[:, :HEAD_DIM],
                                  qb[:, HEAD_DIM:]], axis=0).astype(BF16)
            kh = k_cat[:, h * HEAD_DIM:(h + 1) * HEAD_DIM]
            vh = v_cat[:, h * HEAD_DIM:(h + 1) * HEAD_DIM]
            s = lax.dot_general(q4, kh, (((1,), (1,)), ((), ())), preferred_element_type=F32)
            s = jnp.where(valid, s, NEG_BIG)
            m = jnp.max(s, axis=1, keepdims=True)
            if has_sink:
                sk = jnp.full((4 * blk, 1), sink_ref[4 * h + 3], F32)
                for g in range(3):
                    sk = jnp.where(row4 == g, sink_ref[4 * h + g], sk)
                m = jnp.maximum(m, sk)
            p = jnp.exp(s - m)
            den = jnp.sum(p, axis=1, keepdims=True)
            if has_sink:
                den = den + jnp.exp(sk - m)
            o = jnp.dot(p.astype(BF16), vh, preferred_element_type=F32) / den
            lse = m + jnp.log(den) if want_lse else None
            for half in range(2):
                cb = 2 * h + half
                g0 = 2 * half
                pair = jnp.concatenate(
                    [o[g0 * blk:(g0 + 1) * blk], o[(g0 + 1) * blk:(g0 + 2) * blk]], axis=1)
                if dil > 1:
                    stage[0][cb, rows, :] = pair
                else:
                    o_ref[rows, cb * LANES:(cb + 1) * LANES] = pair.astype(o_ref.dtype)
                if want_lse:
                    lpair = jnp.concatenate(
                        [jnp.broadcast_to(lse[g * blk:(g + 1) * blk], (blk, HEAD_DIM))
                         for g in (g0, g0 + 1)], axis=1)
                    if dil > 1:
                        stage[1][cb, rows, :] = lpair
                    else:
                        lse_ref[rows, cb * LANES:(cb + 1) * LANES] = lpair

    for j in range(nsub):
        if dil == 1:
            one_group(0, j)
        else:
            def body(r, carry, j=j):
                one_group(r, j)
                return carry
            lax.fori_loop(0, dil, body, 0)
    if dil > 1:
        for cb in range(4):
            o_ref[:, cb * LANES:(cb + 1) * LANES] = stage[0][cb].astype(o_ref.dtype)
            if want_lse:
                lse_ref[:, cb * LANES:(cb + 1) * LANES] = stage[1][cb]


def _banded_attention(q, k, v, batch, seq, dil, max_dist, sinks=None, want_lse=False,
                      out_dtype=BF16):
    span = ATTN_BLOCK * dil
    nsub = max(1, 512 // span)
    tb = span * nsub
    nt = seq // tb
    qw, kw = q.shape[1], k.shape[1]
    q3, k3, v3 = (t.reshape(batch, seq, t.shape[1]) for t in (q, k, v))
    kern = functools.partial(_attn_kernel, dil=dil, max_dist=max_dist, nsub=nsub,
                             has_sink=sinks is not None, want_lse=want_lse)
    cur = lambda b, t: (b, t, 0)
    prev = lambda b, t: (b, jnp.maximum(t * nsub - 1, 0), 0)
    in_specs = [pl.BlockSpec((None, tb, LANES), lambda b, t, c=c: (b, t, c)) for c in range(4)] + [
        pl.BlockSpec((None, tb, kw), cur),
        pl.BlockSpec((None, tb, kw), cur),
        pl.BlockSpec((None, span, kw), prev),
        pl.BlockSpec((None, span, kw), prev),
    ]
    args = [q3, q3, q3, q3, k3, v3, k3, v3]
    if sinks is not None:
        in_specs = [pl.BlockSpec(memory_space=pltpu.SMEM)] + in_specs
        args = [sinks.astype(F32)] + args
    n_out = 2 if want_lse else 1
    out_shape = [jax.ShapeDtypeStruct((batch, seq, qw), out_dtype)]
    out_specs = [pl.BlockSpec((None, tb, qw), cur)]
    if want_lse:
        out_shape.append(jax.ShapeDtypeStruct((batch, seq, qw), F32))
        out_specs.append(pl.BlockSpec((None, tb, qw), cur))
    scratch = [pltpu.VMEM((4, tb, LANES), F32)] * n_out if dil > 1 else []
    res = pl.pallas_call(
        kern,
        out_shape=tuple(out_shape),
        grid=(batch, nt),
        in_specs=in_specs,
        out_specs=tuple(out_specs),
        scratch_shapes=scratch,
        compiler_params=_cparams("parallel", "arbitrary"),
        name=f"banded_attn_d{dil}",
    )(*args)
    return tuple(r.reshape(batch * seq, qw) for r in res)


def _merge_kernel(o1, l1, o2, l2, o3, l3, out_ref):
    a, b, c = l1[...], l2[...], l3[...]
    m = jnp.maximum(jnp.maximum(a, b), c)
    ea, eb, ec = jnp.exp(a - m), jnp.exp(b - m), jnp.exp(c - m)
    num = ea * o1[...] + eb * o2[...] + ec * o3[...]
    out_ref[...] = (num / (ea + eb + ec)).astype(out_ref.dtype)


def _merge_patterns(parts, tm=512):
    n, w = parts[0].shape
    spec = pl.BlockSpec((tm, w), lambda i: (i, 0))
    return pl.pallas_call(
        _merge_kernel,
        out_shape=jax.ShapeDtypeStruct((n, w), BF16),
        grid=(n // tm,),
        in_specs=[spec] * 6,
        out_specs=spec,
        compiler_params=_cparams("parallel"),
        name="merge_patterns",
    )(*parts)


def _hgrn_structure():
    c = HGRN_CHUNK
    r = lax.broadcasted_iota(jnp.int32, (c, c), 0)
    j = lax.broadcasted_iota(jnp.int32, (c, c), 1)
    blocks = [(j <= r)]
    for lv in HGRN_LEVELS:
        mid = (r // lv) * lv + lv // 2 - 1
        second = (r % lv) >= lv // 2
        lo = jnp.where(second, mid, r)
        hi = jnp.where(second, r, mid)
        blocks.append((j > lo) & (j <= hi))
    return jnp.concatenate([jnp.where(m, 1.0, 0.0) for m in blocks], axis=0).astype(BF16)


def _split3(x):
    hi = x.astype(BF16)
    r1 = x - hi.astype(F32)
    mid = r1.astype(BF16)
    lo = (r1 - mid.astype(F32)).astype(BF16)
    return hi, mid, lo


def _hgrn_kernel(q_ref, f_ref, i_ref, g_ref, lb_ref, gain_ref, o_ref, st_ref, *, nchunk):
    c = HGRN_CHUNK

    @pl.when(pl.program_id(2) == 0)
    def _():
        st_ref[...] = jnp.zeros_like(st_ref)

    dmat = _hgrn_structure()
    row = lax.broadcasted_iota(jnp.int32, (c, c), 0)
    col = lax.broadcasted_iota(jnp.int32, (c, c), 1)
    lb = jnp.clip(lb_ref[...], 0.0, 1.0)
    gain = gain_ref[...]
    sub_row = lax.broadcasted_iota(jnp.int32, (HGRN_SUB, c), 0)
    sub_col = lax.broadcasted_iota(jnp.int32, (HGRN_SUB, c), 1)

    for ci in range(nchunk):
        rows = pl.ds(ci * c, c)
        q = q_ref[rows, :]
        f = lb + (1.0 - lb) * _sigmoid(f_ref[rows, :])
        lf = jnp.log(jnp.maximum(f, B_MIN_F))
        key = 1.0 - f
        val = i_ref[rows, :].astype(BF16)

        hi, mid, lo = _split3(lf)
        sums = (jnp.dot(dmat, hi, preferred_element_type=F32)
                + jnp.dot(dmat, mid, preferred_element_type=F32)
                + jnp.dot(dmat, lo, preferred_element_type=F32))
        b = sums[0:c]

        att = jnp.zeros((c, c), F32)
        for n, lv in enumerate(HGRN_LEVELS):
            e = jnp.exp(sums[(n + 1) * c:(n + 2) * c])
            second = (row % lv) >= lv // 2
            qs = jnp.where(second, q * e, 0.0).astype(BF16)
            ks = jnp.where(second, 0.0, key * e).astype(BF16)
            a = lax.dot_general(qs, ks, (((1,), (1,)), ((), ())), preferred_element_type=F32)
            att = att + jnp.where((row // lv) == (col // lv), a, 0.0)

        diag = []
        for a0 in range(0, c, HGRN_SUB):
            qa, ka, ba = q[a0:a0 + HGRN_SUB], key[a0:a0 + HGRN_SUB], b[a0:a0 + HGRN_SUB]
            acc = jnp.zeros((HGRN_SUB, c), F32)
            for s in range(HGRN_SUB):
                e = jnp.exp(jnp.minimum(ba - ba[s:s + 1, :], 0.0))
                w = jnp.sum(qa * e * ka[s:s + 1, :], axis=1, keepdims=True)
                acc = jnp.where((sub_col == a0 + s) & (sub_row >= s), w, acc)
            diag.append(acc)
        att = att + jnp.concatenate(diag, axis=0)

        st = st_ref[...]
        b_last = b[c - 1:c, :]
        q_in = (q * jnp.exp(b)).astype(BF16)
        k_out = (key * jnp.exp(b_last - b)).astype(BF16)
        o = jnp.dot(att.astype(BF16), val, preferred_element_type=F32)
        o = o + lax.dot_general(q_in, st.astype(BF16), (((1,), (1,)), ((), ())),
                                preferred_element_type=F32)
        st_ref[...] = jnp.exp(b_last) * st + lax.dot_general(
            val, k_out, (((0,), (0,)), ((), ())), preferred_element_type=F32)

        o = o * lax.rsqrt(jnp.mean(o * o, axis=1, keepdims=True) + NORM_EPS)
        g = g_ref[rows, :]
        o_ref[rows, :] = (o * gain * (g * _sigmoid(g))).astype(o_ref.dtype)


def _hgrn2(q, f, i_in, g, lb, gain, batch, seq, nchunk=4):
    width = q.shape[1]
    heads = width // B_KEY_DIM
    tb = HGRN_CHUNK * nchunk
    q3, f3, i3, g3 = (t.reshape(batch, seq, width) for t in (q, f, i_in, g))
    blk = pl.BlockSpec((None, tb, B_KEY_DIM), lambda b, h, t: (b, t, h))
    vec = pl.BlockSpec((1, B_KEY_DIM), lambda b, h, t: (0, h))
    out = pl.pallas_call(
        functools.partial(_hgrn_kernel, nchunk=nchunk),
        out_shape=jax.ShapeDtypeStruct((batch, seq, width), BF16),
        grid=(batch, heads, seq // tb),
        in_specs=[blk, blk, blk, blk, vec, vec],
        out_specs=blk,
        scratch_shapes=[pltpu.VMEM((B_KEY_DIM, B_KEY_DIM), F32)],
        compiler_params=_cparams("parallel", "parallel", "arbitrary"),
        name="hgrn2",
    )(q3, f3, i3, g3, lb.reshape(1, width).astype(F32), gain.reshape(1, width).astype(F32))
    return out.reshape(batch * seq, width)


def _retention_kernel(q_ref, k_ref, v_ref, g_ref, gain_ref, o_ref, st_ref, *, nchunk):
    c = ATTN_BLOCK

    @pl.when(pl.program_id(1) == 0)
    def _():
        st_ref[...] = jnp.zeros_like(st_ref)

    ri = lax.broadcasted_iota(jnp.int32, (c, c), 0)
    ci_ = lax.broadcasted_iota(jnp.int32, (c, c), 1)
    rel = (ri - ci_).astype(F32)
    rowf = ri.astype(F32)

    for h in range(C_HEADS):
        log_gamma = math.log1p(-(2.0 ** (-5.0 - h)))
        decay = jnp.where(rel >= 0.0, jnp.exp(log_gamma * jnp.maximum(rel, 0.0)), 0.0)
        head_scale = jnp.exp(log_gamma * (rowf + 1.0))
        tail_scale = jnp.exp(log_gamma * (c - 1.0 - rowf))
        chunk_decay = math.exp(log_gamma * c)
        gain = gain_ref[:, h * C_VAL_DIM:(h + 1) * C_VAL_DIM]
        for ci in range(nchunk):
            rows = pl.ds(ci * c, c)
            q = q_ref[rows, h * C_KEY_DIM:(h + 1) * C_KEY_DIM].astype(BF16)
            k = k_ref[rows, h * C_KEY_DIM:(h + 1) * C_KEY_DIM].astype(BF16)
            v = v_ref[rows, h * C_VAL_DIM:(h + 1) * C_VAL_DIM]
            st = st_ref[h]
            s = lax.dot_general(q, k, (((1,), (1,)), ((), ())), preferred_element_type=F32)
            o = jnp.dot((s * decay).astype(BF16), v.astype(BF16), preferred_element_type=F32)
            o = o + head_scale * jnp.dot(q, st.astype(BF16), preferred_element_type=F32)
            st_ref[h] = chunk_decay * st + lax.dot_general(
                k, (v * tail_scale).astype(BF16), (((0,), (0,)), ((), ())),
                preferred_element_type=F32)
            o = o - jnp.mean(o, axis=1, keepdims=True)
            o = o * lax.rsqrt(jnp.mean(o * o, axis=1, keepdims=True) + NORM_EPS)
            g = g_ref[rows, h * C_VAL_DIM:(h + 1) * C_VAL_DIM]
            o_ref[rows, h * C_VAL_DIM:(h + 1) * C_VAL_DIM] = (
                o * gain * (g * _sigmoid(g))).astype(o_ref.dtype)


def _retention(q, k, v, g, gain, batch, seq, nchunk=4):
    tb = ATTN_BLOCK * nchunk
    kw, vw = q.shape[1], v.shape[1]
    q3, k3 = q.reshape(batch, seq, kw), k.reshape(batch, seq, kw)
    v3, g3 = v.reshape(batch, seq, vw), g.reshape(batch, seq, vw)
    kspec = pl.BlockSpec((None, tb, kw), lambda b, t: (b, t, 0))
    vspec = pl.BlockSpec((None, tb, vw), lambda b, t: (b, t, 0))
    out = pl.pallas_call(
        functools.partial(_retention_kernel, nchunk=nchunk),
        out_shape=jax.ShapeDtypeStruct((batch, seq, vw), BF16),
        grid=(batch, seq // tb),
        in_specs=[kspec, kspec, vspec, vspec, pl.BlockSpec((1, vw), lambda b, t: (0, 0))],
        out_specs=vspec,
        scratch_shapes=[pltpu.VMEM((C_HEADS, C_KEY_DIM, C_VAL_DIM), F32)],
        compiler_params=_cparams("parallel", "arbitrary"),
        name="retention",
    )(q3, k3, v3, g3, gain.reshape(1, vw).astype(F32))
    return out.reshape(batch * seq, vw)


def _layer_norm(z, g, b):
    mu = jnp.mean(z, axis=1, keepdims=True)
    zc = z - mu
    var = jnp.mean(zc * zc, axis=1, keepdims=True)
    return zc * lax.rsqrt(var + LN_EPS) * g + b


def _outproj_kernel(ya_ref, yb_ref, w_ref, x_ref, g_ref, b_ref, rwh_ref, rwl_ref, rb_ref,
                    x1_ref, idx_ref, gate_ref, *, alpha):
    half = ya_ref.shape[1]
    mix = jnp.dot(ya_ref[...], w_ref[0:half, :], preferred_element_type=F32)
    mix = mix + jnp.dot(yb_ref[...], w_ref[half:, :], preferred_element_type=F32)
    x1 = _layer_norm(alpha * x_ref[...] + mix, g_ref[...], b_ref[...])
    x1_ref[...] = x1

    xh = x1.astype(BF16)
    xl = (x1 - xh.astype(F32)).astype(BF16)
    logits = (jnp.dot(xh, rwh_ref[...], preferred_element_type=F32)
              + jnp.dot(xh, rwl_ref[...], preferred_element_type=F32)
              + jnp.dot(xl, rwh_ref[...], preferred_element_type=F32)) + rb_ref[...]
    lane = lax.broadcasted_iota(jnp.int32, logits.shape, 1)
    idx_acc = jnp.zeros(logits.shape, jnp.int32)
    val_acc = jnp.zeros(logits.shape, F32)
    top0 = None
    den = None
    for kk in range(TOP_K):
        m = jnp.max(logits, axis=1, keepdims=True)
        sel = jnp.min(jnp.where(logits == m, lane, LANES), axis=1, keepdims=True)
        if kk == 0:
            top0 = m
        e = jnp.exp(m - top0)
        den = e if den is None else den + e
        idx_acc = jnp.where(lane == kk, sel, idx_acc)
        val_acc = jnp.where(lane == kk, e, val_acc)
        logits = jnp.where(lane == sel, NEG_BIG * 2.0, logits)
    idx_ref[...] = idx_acc[:, 0:idx_ref.shape[1]]
    gate_ref[...] = (val_acc / den)[:, 0:gate_ref.shape[1]]


def _outproj_norm_route(ya, yb, w_out, x, ln_g, ln_b, rw, rb, alpha, tm=256):
    n, d = x.shape
    half = ya.shape[1]
    rw_pad = jnp.zeros((d, LANES), F32).at[:, :N_EXPERTS].set(rw.astype(F32))
    rwh = rw_pad.astype(BF16)
    rwl = (rw_pad - rwh.astype(F32)).astype(BF16)
    rb_pad = jnp.full((1, LANES), NEG_BIG, F32).at[0, :N_EXPERTS].set(rb.astype(F32))
    row = lambda i: (i, 0)
    fixed = lambda i: (0, 0)
    return pl.pallas_call(
        functools.partial(_outproj_kernel, alpha=alpha),
        out_shape=(jax.ShapeDtypeStruct((n, d), F32),
                   jax.ShapeDtypeStruct((n, 8), jnp.int32),
                   jax.ShapeDtypeStruct((n, 8), F32)),
        grid=(n // tm,),
        in_specs=[
            pl.BlockSpec((tm, half), row),
            pl.BlockSpec((tm, half), row),
            pl.BlockSpec((2 * half, d), fixed),
            pl.BlockSpec((tm, d), row),
            pl.BlockSpec((1, d), fixed),
            pl.BlockSpec((1, d), fixed),
            pl.BlockSpec((d, LANES), fixed),
            pl.BlockSpec((d, LANES), fixed),
            pl.BlockSpec((1, LANES), fixed),
        ],
        out_specs=(pl.BlockSpec((tm, d), row),
                   pl.BlockSpec((tm, 8), row),
                   pl.BlockSpec((tm, 8), row)),
        compiler_params=_cparams("parallel"),
        name="outproj_norm_route",
    )(ya, yb, w_out.astype(BF16), x, ln_g.reshape(1, d), ln_b.reshape(1, d), rwh, rwl, rb_pad)


def _gather_rows_kernel(idx_hbm, src_hbm, out_ref, idx_smem, idx_sem, row_sem, *, rows):
    i = pl.program_id(0)
    cp = pltpu.make_async_copy(idx_hbm.at[i], idx_smem, idx_sem)
    cp.start()
    cp.wait()

    def row_copy(r):
        return pltpu.make_async_copy(src_hbm.at[idx_smem[r]], out_ref.at[r], row_sem)

    def issue(r, carry):
        row_copy(r).start()
        return carry

    def drain(r, carry):
        row_copy(r).wait()
        return carry

    lax.fori_loop(0, rows, issue, 0)
    lax.fori_loop(0, rows, drain, 0)


def _gather_rows(src, idx, rows=MOE_BLOCK):
    nblocks = idx.shape[0]
    d = src.shape[1]
    return pl.pallas_call(
        functools.partial(_gather_rows_kernel, rows=rows),
        out_shape=jax.ShapeDtypeStruct((nblocks * rows, d), src.dtype),
        grid=(nblocks,),
        in_specs=[pl.BlockSpec(memory_space=pl.ANY), pl.BlockSpec(memory_space=pl.ANY)],
        out_specs=pl.BlockSpec((rows, d), lambda i: (i, 0)),
        scratch_shapes=[pltpu.SMEM((rows,), jnp.int32),
                        pltpu.SemaphoreType.DMA(()),
                        pltpu.SemaphoreType.DMA(())],
        compiler_params=_cparams("arbitrary"),
        name="gather_rows",
    )(idx, src)


def _moe_kernel(be_ref, nv_ref, xs_ref, wgu_ref, bgu_ref, wdn_ref, bdn_ref, y_ref, act_ref,
                *, chunk):
    i = pl.program_id(0)
    de = wdn_ref.shape[0]

    @pl.when(nv_ref[i] > 0)
    def _():
        xb = xs_ref[...].astype(BF16)
        for c0 in range(0, de, chunk):
            gate = jnp.dot(xb, wgu_ref[:, c0:c0 + chunk], preferred_element_type=F32)
            gate = gate + bgu_ref[:, c0:c0 + chunk]
            up = jnp.dot(xb, wgu_ref[:, de + c0:de + c0 + chunk], preferred_element_type=F32)
            up = up + bgu_ref[:, de + c0:de + c0 + chunk]
            gate = jnp.minimum(gate, SWIGLU_LIMIT)
            up = jnp.clip(up, -SWIGLU_LIMIT, SWIGLU_LIMIT)
            act = (up + 1.0) * gate * _sigmoid(SWIGLU_ALPHA * gate)
            act_ref[:, c0:c0 + chunk] = act.astype(BF16)
        y_ref[...] = jnp.dot(act_ref[...], wdn_ref[...], preferred_element_type=F32) + bdn_ref[...]

    @pl.when(nv_ref[i] == 0)
    def _():
        y_ref[...] = jnp.zeros_like(y_ref)


def _moe_experts(xs, block_e, nvalid, w_gu, b_gu, w_dn, b_dn, chunk=512):
    cap, d = xs.shape
    de = w_dn.shape[1]
    nblocks = cap // MOE_BLOCK
    grid_spec = pltpu.PrefetchScalarGridSpec(
        num_scalar_prefetch=2,
        grid=(nblocks,),
        in_specs=[
            pl.BlockSpec((MOE_BLOCK, d), lambda i, be, nv: (i, 0)),
            pl.BlockSpec((None, d, 2 * de), lambda i, be, nv: (be[i], 0, 0)),
            pl.BlockSpec((None, 1, 2 * de), lambda i, be, nv: (be[i], 0, 0)),
            pl.BlockSpec((None, de, d), lambda i, be, nv: (be[i], 0, 0)),
            pl.BlockSpec((None, 1, d), lambda i, be, nv: (be[i], 0, 0)),
        ],
        out_specs=pl.BlockSpec((MOE_BLOCK, d), lambda i, be, nv: (i, 0)),
        scratch_shapes=[pltpu.VMEM((MOE_BLOCK, de), BF16)],
    )
    return pl.pallas_call(
        functools.partial(_moe_kernel, chunk=chunk),
        out_shape=jax.ShapeDtypeStruct((cap, d), F32),
        grid_spec=grid_spec,
        compiler_params=_cparams("arbitrary"),
        name="moe_experts",
    )(block_e, nvalid, xs, w_gu.astype(BF16), b_gu.reshape(N_EXPERTS, 1, 2 * de),
      w_dn.astype(BF16), b_dn.reshape(N_EXPERTS, 1, d))


def _combine_kernel(pos_hbm, y_hbm, x_ref, gate_ref, g_ref, b_ref, o_ref,
                    buf_ref, pos_smem, pos_sem, row_sem, *, tm, alpha):
    i = pl.program_id(0)
    cp = pltpu.make_async_copy(pos_hbm.at[i], pos_smem, pos_sem)
    cp.start()
    cp.wait()

    def row_copy(r):
        return pltpu.make_async_copy(y_hbm.at[pos_smem[r]], buf_ref.at[r], row_sem)

    def issue(r, carry):
        row_copy(r).start()
        return carry

    def drain(r, carry):
        row_copy(r).wait()
        return carry

    lax.fori_loop(0, TOP_K * tm, issue, 0)
    lax.fori_loop(0, TOP_K * tm, drain, 0)

    gates = gate_ref[...]
    ffn = jnp.zeros(o_ref.shape, F32)
    for kk in range(TOP_K):
        ffn = ffn + gates[:, kk:kk + 1] * buf_ref[kk * tm:(kk + 1) * tm, :]
    o_ref[...] = _layer_norm(alpha * x_ref[...] + ffn, g_ref[...], b_ref[...])


def _combine_norm(y, pos, x1, gates, ln_g, ln_b, alpha, tm=128):
    n, d = x1.shape
    steps = n // tm
    pos_t = pos.reshape(steps, tm, TOP_K).transpose(0, 2, 1).reshape(steps, TOP_K * tm)
    row = lambda i: (i, 0)
    fixed = lambda i: (0, 0)
    return pl.pallas_call(
        functools.partial(_combine_kernel, tm=tm, alpha=alpha),
        out_shape=jax.ShapeDtypeStruct((n, d), F32),
        grid=(steps,),
        in_specs=[
            pl.BlockSpec(memory_space=pl.ANY),
            pl.BlockSpec(memory_space=pl.ANY),
            pl.BlockSpec((tm, d), row),
            pl.BlockSpec((tm, 8), row),
            pl.BlockSpec((1, d), fixed),
            pl.BlockSpec((1, d), fixed),
        ],
        out_specs=pl.BlockSpec((tm, d), row),
        scratch_shapes=[pltpu.VMEM((TOP_K * tm, d), F32),
                        pltpu.SMEM((TOP_K * tm,), jnp.int32),
                        pltpu.SemaphoreType.DMA(()),
                        pltpu.SemaphoreType.DMA(())],
        compiler_params=_cparams("arbitrary"),
        name="combine_norm",
    )(pos_t, y, x1, gates, ln_g.reshape(1, d), ln_b.reshape(1, d))


def _routing_tables(top_idx, n):
    nk = n * TOP_K
    flat_e = top_idx[:, :TOP_K].reshape(nk)
    onehot = (flat_e[:, None] == jnp.arange(N_EXPERTS, dtype=jnp.int32)[None, :]).astype(jnp.int32)
    csum = jnp.cumsum(onehot, axis=0)
    rank = jnp.take_along_axis(csum, flat_e[:, None], axis=1)[:, 0] - 1
    counts = csum[-1]
    padded = (counts + MOE_BLOCK - 1) // MOE_BLOCK * MOE_BLOCK
    pad_end = jnp.cumsum(padded)
    pad_start = pad_end - padded
    start = jnp.cumsum(counts) - counts
    pos = (pad_start[flat_e] + rank).astype(jnp.int32)
    n_blocks = nk // MOE_BLOCK + N_EXPERTS + 1
    blk_first = jnp.arange(n_blocks, dtype=jnp.int32) * MOE_BLOCK
    block_e = jnp.minimum(jnp.searchsorted(pad_end, blk_first, side="right"),
                          N_EXPERTS - 1).astype(jnp.int32)
    nvalid = jnp.clip(counts[block_e] - (blk_first - pad_start[block_e]), 0, MOE_BLOCK)
    nvalid = jnp.where(blk_first < pad_end[-1], nvalid, 0).astype(jnp.int32)
    order = jnp.argsort(flat_e, stable=True).astype(jnp.int32)
    slot = jnp.arange(n_blocks * MOE_BLOCK, dtype=jnp.int32)
    slot_e = jnp.repeat(block_e, MOE_BLOCK)
    within = slot - pad_start[slot_e]
    live = (within < counts[slot_e]) & (slot < pad_end[-1])
    src = order[jnp.clip(start[slot_e] + within, 0, nk - 1)] // TOP_K
    rows = jnp.where(live, src, 0).astype(jnp.int32)
    return rows.reshape(n_blocks, MOE_BLOCK), pos.reshape(n, TOP_K), block_e, nvalid


_EVEN_PLAN = (
    (512, "attn", HEAD_DIM ** -0.5), (128, "attn", 1.0), (128, None, 1.0),
    (512, None, 1.0), (512, None, 1.0), (512, None, 1.0), (512, None, 1.0),
)
_ODD_PLAN = (
    (256, "ret", 1.0), (256, "ret", C_KEY_DIM ** -0.5), (512, None, 1.0), (512, None, 1.0),
) + ((512, "attn", HEAD_DIM ** -0.5), (128, "attn", 1.0), (128, None, 1.0)) * len(D_PATTERNS)


def kernel(x, w_in_even, b_in_even, attn_sinks, hgrn_lb_logits, hgrn_norm, w_out_even, w_in_odd,
           b_in_odd, ret_norm, w_out_odd, ln1_g, ln1_b, ln2_g, ln2_b, router_w, router_b,
           expert_w_gu, expert_b_gu, expert_w_dn, expert_b_dn):
    batch, seq, d = x.shape
    n = batch * seq
    depth = ln1_g.shape[0]
    alpha = float((2 * depth) ** 0.25)

    rope_inv = 1.0 / (ROPE_THETA ** (jnp.arange(0, ROPE_DIM, 2, dtype=F32) / ROPE_DIM))
    ret_inv = 1.0 / (RET_THETA ** jnp.linspace(0.0, 1.0, C_KEY_DIM // 2, dtype=F32))
    tab_attn = _rope_tables(seq, ROPE_DIM // 2, rope_inv, HEAD_DIM)
    tab_ret = _rope_tables(seq, C_KEY_DIM // 2, ret_inv, C_KEY_DIM)
    lb_soft = jax.nn.softmax(hgrn_lb_logits.astype(F32), axis=0)
    lower_bounds = jnp.concatenate(
        [jnp.zeros_like(lb_soft[:1]), jnp.cumsum(lb_soft, axis=0)[:-1]], axis=0)

    xf = x.reshape(n, d)
    for layer in range(depth):
        j = layer // 2
        if layer % 2 == 0:
            aq, ak, av, bq, bf, bi, bg = _inproj(
                xf, w_in_even[j].astype(BF16), b_in_even[j].reshape(1, -1), tab_attn, tab_ret,
                _EVEN_PLAN, seq)
            (ya,) = _banded_attention(aq, ak, av, batch, seq, 1, A_WINDOW - 1,
                                      sinks=attn_sinks[j])
            yb = _hgrn2(bq, bf, bi, bg, lower_bounds[j], hgrn_norm[j], batch, seq)
            w_out = w_out_even[j]
        else:
            outs = _inproj(xf, w_in_odd[j].astype(BF16), b_in_odd[j].reshape(1, -1), tab_attn,
                           tab_ret, _ODD_PLAN, seq)
            cq, ck, cv, cg = outs[:4]
            ya = _retention(cq, ck, cv, cg, ret_norm[j], batch, seq)
            parts = []
            for p, (window, dil) in enumerate(D_PATTERNS):
                dq, dk, dv = outs[4 + 3 * p:7 + 3 * p]
                o, lse = _banded_attention(dq, dk, dv, batch, seq, dil, window // dil,
                                           want_lse=True, out_dtype=F32)
                parts += [o, lse]
            yb = _merge_patterns(parts)
            w_out = w_out_odd[j]
        x1, top_idx, gates = _outproj_norm_route(
            ya, yb, w_out, xf, ln1_g[layer], ln1_b[layer], router_w[layer], router_b[layer], alpha)
        rows, pos, block_e, nvalid = _routing_tables(top_idx, n)
        xs = _gather_rows(x1, rows)
        y = _moe_experts(xs, block_e, nvalid, expert_w_gu[layer], expert_b_gu[layer],
                         expert_w_dn[layer], expert_b_dn[layer])
        xf = _combine_norm(y, pos, x1, gates, ln2_g[layer], ln2_b[layer], alpha)
    return xf.reshape(batch, seq, d)
```

```python
import functools
import math

import jax
import jax.numpy as jnp
from jax import lax
from jax.experimental import pallas as pl
from jax.experimental.pallas import tpu as pltpu

F32 = jnp.float32
BF16 = jnp.bfloat16

HEAD_DIM = 64
ATTN_BLOCK = 128
A_Q_HEADS = 8
A_KV_HEADS = 2
A_WINDOW = 128
B_HEADS = 4
B_KEY_DIM = 128
B_MIN_F = 1e-30
C_HEADS = 4
C_KEY_DIM = 64
C_VAL_DIM = 128
D_PATTERNS = ((128, 1), (512, 4), (2048, 16))
ROPE_THETA = 500000.0
ROPE_DIM = HEAD_DIM // 4
RET_THETA = 10000.0
N_EXPERTS = 32
TOP_K = 4
SWIGLU_LIMIT = 7.0
SWIGLU_ALPHA = 1.702
MOE_BLOCK = 512
LN_EPS = 1e-5
NORM_EPS = 1e-6
NEG_BIG = -1e30

LANES = 128
SUBLANES = 8
VMEM_LIMIT = 48 * 1024 * 1024
MOE_VMEM_LIMIT = 56 * 1024 * 1024

HGRN_CHUNK = 128
HGRN_SUB = 8
HGRN_LEVELS = (16, 32, 64, 128)


def _cparams(*sem):
    return pltpu.CompilerParams(dimension_semantics=sem, vmem_limit_bytes=VMEM_LIMIT)


def _to_tiles(ref, value):
    rows = value.shape[0]
    for c in range(SUBLANES):
        ref[pl.ds(c, rows, stride=SUBLANES), :] = value[:, c * LANES:(c + 1) * LANES]


def _from_tiles(ref, rows, base=0):
    return jnp.concatenate(
        [ref[pl.ds(base + c, rows, stride=SUBLANES), :] for c in range(SUBLANES)], axis=1)


def _sigmoid(z):
    return 1.0 / (1.0 + jnp.exp(-z))


def _rotate(h, cos, sa, sb, shift):
    width = h.shape[1]
    rep = width // LANES
    if rep > 1:
        cos = jnp.concatenate([cos] * rep, axis=1)
        sa = jnp.concatenate([sa] * rep, axis=1)
        sb = jnp.concatenate([sb] * rep, axis=1)
    up = pltpu.roll(h, width - shift, axis=1)
    dn = pltpu.roll(h, shift, axis=1)
    return h * cos + up * sa + dn * sb


def _inproj_kernel(x_ref, w_ref, b_ref, ta_ref, tr_ref, *out_refs, plan):
    xb = x_ref[...].astype(BF16)
    off = 0
    for o_ref, (width, rope, scale) in zip(out_refs, plan):
        h = jnp.dot(xb, w_ref[:, off:off + width], preferred_element_type=F32)
        h = h + b_ref[:, off:off + width]
        if rope == "attn":
            h = _rotate(h, ta_ref[0], ta_ref[1], ta_ref[2], ROPE_DIM // 2)
        elif rope == "ret":
            h = _rotate(h, tr_ref[0], tr_ref[1], tr_ref[2], C_KEY_DIM // 2)
        if scale != 1.0:
            h = h * scale
        o_ref[...] = h.astype(o_ref.dtype)
        off += width


def _inproj(x, w, b, tab_attn, tab_ret, plan, seq, tm=256):
    n, d = x.shape
    cols = w.shape[1]
    tpb = seq // tm
    kern = functools.partial(_inproj_kernel, plan=plan)
    outs = tuple(jax.ShapeDtypeStruct((n, wd), F32) for wd, _, _ in plan)
    return pl.pallas_call(
        kern,
        out_shape=outs,
        grid=(n // tm,),
        in_specs=[
            pl.BlockSpec((tm, d), lambda i: (i, 0)),
            pl.BlockSpec((d, cols), lambda i: (0, 0)),
            pl.BlockSpec((1, cols), lambda i: (0, 0)),
            pl.BlockSpec((3, tm, LANES), lambda i: (0, i % tpb, 0)),
            pl.BlockSpec((3, tm, LANES), lambda i: (0, i % tpb, 0)),
        ],
        out_specs=tuple(pl.BlockSpec((tm, wd), lambda i: (i, 0)) for wd, _, _ in plan),
        compiler_params=_cparams("parallel"),
        name="inproj",
    )(x, w, b, tab_attn, tab_ret)


def _rope_tables(seq, half, inv_freq, period):
    pos = jnp.arange(seq, dtype=F32)
    ang = pos[:, None] * inv_freq[None, :]
    cos, sin = jnp.cos(ang), jnp.sin(ang)
    pad = period - 2 * half
    ones = jnp.ones((seq, pad), F32)
    zeros = jnp.zeros((seq, pad), F32)
    zh = jnp.zeros((seq, half), F32)
    cosf = jnp.concatenate([cos, cos, ones], axis=1)
    sa = jnp.concatenate([-sin, zh, zeros], axis=1)
    sb = jnp.concatenate([zh, sin, zeros], axis=1)
    rep = LANES // period
    return jnp.stack([jnp.tile(t, (1, rep)) for t in (cosf, sa, sb)], axis=0)


def _attn_kernel(*refs, dil, max_dist, nsub, has_sink, want_lse):
    if has_sink:
        sink_ref, refs = refs[0], refs[1:]
    q_refs = refs[:4]
    k_ref, v_ref, kp_ref, vp_ref = refs[4:8]
    o_ref = refs[8]
    lse_ref = refs[9] if want_lse else None
    n_out = 2 if want_lse else 1
    stage = refs[8 + n_out:]
    blk = ATTN_BLOCK
    qi = lax.broadcasted_iota(jnp.int32, (4 * blk, 2 * blk), 0) % blk
    kj = lax.broadcasted_iota(jnp.int32, (4 * blk, 2 * blk), 1)
    dist = qi - kj + blk
    band = (dist >= 0) & (dist <= max_dist)
    first_kmin = jnp.where(pl.program_id(1) == 0, blk, 0)
    band_first = band & (kj >= first_kmin)
    row4 = lax.broadcasted_iota(jnp.int32, (4 * blk, 1), 0) // blk

    def take(start):
        return pl.ds(start, blk, stride=dil) if dil > 1 else pl.ds(start, blk)

    def one_group(r, j):
        rows = take(j * blk * dil + r)
        if j == 0:
            k_prev, v_prev = kp_ref[take(r), :], vp_ref[take(r), :]
            valid = band_first
        else:
            prow = take((j - 1) * blk * dil + r)
            k_prev, v_prev = k_ref[prow, :], v_ref[prow, :]
            valid = band
        k_cat = jnp.concatenate([k_prev, k_ref[rows, :]], axis=0).astype(BF16)
        v_cat = jnp.concatenate([v_prev, v_ref[rows, :]], axis=0).astype(BF16)
        for h in range(A_KV_HEADS):
            qa, qb = q_refs[2 * h][rows, :], q_refs[2 * h + 1][rows, :]
            q4 = jnp.concatenate([qa[:, :HEAD_DIM], qa[:, HEAD_DIM:], qb[:, :HEAD_DIM],
                                  qb[:, HEAD_DIM:]], axis=0).astype(BF16)
            kh = k_cat[:, h * HEAD_DIM:(h + 1) * HEAD_DIM]
            vh = v_cat[:, h * HEAD_DIM:(h + 1) * HEAD_DIM]
            s = lax.dot_general(q4, kh, (((1,), (1,)), ((), ())), preferred_element_type=F32)
            s = jnp.where(valid, s, NEG_BIG)
            m = jnp.max(s, axis=1, keepdims=True)
            if has_sink:
                sk = jnp.full((4 * blk, 1), sink_ref[4 * h + 3], F32)
                for g in range(3):
                    sk = jnp.where(row4 == g, sink_ref[4 * h + g], sk)
                m = jnp.maximum(m, sk)
            p = jnp.exp(s - m)
            den = jnp.sum(p, axis=1, keepdims=True)
            if has_sink:
                den = den + jnp.exp(sk - m)
            o = jnp.dot(p.astype(BF16), vh, preferred_element_type=F32) / den
            lse = m + jnp.log(den) if want_lse else None
            for half in range(2):
                cb = 2 * h + half
                g0 = 2 * half
                pair = jnp.concatenate(
                    [o[g0 * blk:(g0 + 1) * blk], o[(g0 + 1) * blk:(g0 + 2) * blk]], axis=1)
                if dil > 1:
                    stage[0][cb, rows, :] = pair
                else:
                    o_ref[rows, cb * LANES:(cb + 1) * LANES] = pair.astype(o_ref.dtype)
                if want_lse:
                    lpair = jnp.concatenate(
                        [jnp.broadcast_to(lse[g * blk:(g + 1) * blk], (blk, HEAD_DIM))
                         for g in (g0, g0 + 1)], axis=1)
                    if dil > 1:
                        stage[1][cb, rows, :] = lpair
                    else:
                        lse_ref[rows, cb * LANES:(cb + 1) * LANES] = lpair

    for j in range(nsub):
        if dil == 1:
            one_group(0, j)
        else:
            def body(r, carry, j=j):
                one_group(r, j)
                return carry
            lax.fori_loop(0, dil, body, 0)
    if dil > 1:
        for cb in range(4):
            o_ref[:, cb * LANES:(cb + 1) * LANES] = stage[0][cb].astype(o_ref.dtype)
            if want_lse:
                lse_ref[:, cb * LANES:(cb + 1) * LANES] = stage[1][cb]


def _banded_attention(q, k, v, batch, seq, dil, max_dist, sinks=None, want_lse=False,
                      out_dtype=BF16):
    span = ATTN_BLOCK * dil
    nsub = max(1, 512 // span)
    tb = span * nsub
    nt = seq // tb
    qw, kw = q.shape[1], k.shape[1]
    q3, k3, v3 = (t.reshape(batch, seq, t.shape[1]) for t in (q, k, v))
    kern = functools.partial(_attn_kernel, dil=dil, max_dist=max_dist, nsub=nsub,
                             has_sink=sinks is not None, want_lse=want_lse)
    cur = lambda b, t: (b, t, 0)
    prev = lambda b, t: (b, jnp.maximum(t * nsub - 1, 0), 0)
    in_specs = [pl.BlockSpec((None, tb, LANES), lambda b, t, c=c: (b, t, c)) for c in range(4)] + [
        pl.BlockSpec((None, tb, kw), cur),
        pl.BlockSpec((None, tb, kw), cur),
        pl.BlockSpec((None, span, kw), prev),
        pl.BlockSpec((None, span, kw), prev),
    ]
    args = [q3, q3, q3, q3, k3, v3, k3, v3]
    if sinks is not None:
        in_specs = [pl.BlockSpec(memory_space=pltpu.SMEM)] + in_specs
        args = [sinks.astype(F32)] + args
    n_out = 2 if want_lse else 1
    out_shape = [jax.ShapeDtypeStruct((batch, seq, qw), out_dtype)]
    out_specs = [pl.BlockSpec((None, tb, qw), cur)]
    if want_lse:
        out_shape.append(jax.ShapeDtypeStruct((batch, seq, qw), F32))
        out_specs.append(pl.BlockSpec((None, tb, qw), cur))
    scratch = [pltpu.VMEM((4, tb, LANES), F32)] * n_out if dil > 1 else []
    res = pl.pallas_call(
        kern,
        out_shape=tuple(out_shape),
        grid=(batch, nt),
        in_specs=in_specs,
        out_specs=tuple(out_specs),
        scratch_shapes=scratch,
        compiler_params=_cparams("parallel", "arbitrary"),
        name=f"banded_attn_d{dil}",
    )(*args)
    return tuple(r.reshape(batch * seq, qw) for r in res)


def _merge_kernel(o1, l1, o2, l2, o3, l3, out_ref):
    a, b, c = l1[...], l2[...], l3[...]
    m = jnp.maximum(jnp.maximum(a, b), c)
    ea, eb, ec = jnp.exp(a - m), jnp.exp(b - m), jnp.exp(c - m)
    num = ea * o1[...] + eb * o2[...] + ec * o3[...]
    out_ref[...] = (num / (ea + eb + ec)).astype(out_ref.dtype)


def _merge_patterns(parts, tm=512):
    n, w = parts[0].shape
    spec = pl.BlockSpec((tm, w), lambda i: (i, 0))
    return pl.pallas_call(
        _merge_kernel,
        out_shape=jax.ShapeDtypeStruct((n, w), BF16),
        grid=(n // tm,),
        in_specs=[spec] * 6,
        out_specs=spec,
        compiler_params=_cparams("parallel"),
        name="merge_patterns",
    )(*parts)


def _hgrn_structure():
    c = HGRN_CHUNK
    r = lax.broadcasted_iota(jnp.int32, (c, c), 0)
    j = lax.broadcasted_iota(jnp.int32, (c, c), 1)
    blocks = [(j <= r)]
    for lv in HGRN_LEVELS:
        mid = (r // lv) * lv + lv // 2 - 1
        second = (r % lv) >= lv // 2
        lo = jnp.where(second, mid, r)
        hi = jnp.where(second, r, mid)
        blocks.append((j > lo) & (j <= hi))
    return jnp.concatenate([jnp.where(m, 1.0, 0.0) for m in blocks], axis=0).astype(BF16)


def _split3(x):
    hi = x.astype(BF16)
    r1 = x - hi.astype(F32)
    mid = r1.astype(BF16)
    lo = (r1 - mid.astype(F32)).astype(BF16)
    return hi, mid, lo


def _hgrn_kernel(q_ref, f_ref, i_ref, g_ref, lb_ref, gain_ref, o_ref, st_ref, *, nchunk):
    c = HGRN_CHUNK

    @pl.when(pl.program_id(2) == 0)
    def _():
        st_ref[...] = jnp.zeros_like(st_ref)

    dmat = _hgrn_structure()
    row = lax.broadcasted_iota(jnp.int32, (c, c), 0)
    col = lax.broadcasted_iota(jnp.int32, (c, c), 1)
    lb = jnp.clip(lb_ref[...], 0.0, 1.0)
    gain = gain_ref[...]
    sub_row = lax.broadcasted_iota(jnp.int32, (HGRN_SUB, c), 0)
    sub_col = lax.broadcasted_iota(jnp.int32, (HGRN_SUB, c), 1)

    for ci in range(nchunk):
        rows = pl.ds(ci * c, c)
        q = q_ref[rows, :]
        f = lb + (1.0 - lb) * _sigmoid(f_ref[rows, :])
        lf = jnp.log(jnp.maximum(f, B_MIN_F))
        key = 1.0 - f
        val = i_ref[rows, :].astype(BF16)

        hi, mid, lo = _split3(lf)
        sums = (jnp.dot(dmat, hi, preferred_element_type=F32)
                + jnp.dot(dmat, mid, preferred_element_type=F32)
                + jnp.dot(dmat, lo, preferred_element_type=F32))
        b = sums[0:c]

        att = jnp.zeros((c, c), F32)
        for n, lv in enumerate(HGRN_LEVELS):
            e = jnp.exp(sums[(n + 1) * c:(n + 2) * c])
            second = (row % lv) >= lv // 2
            qs = jnp.where(second, q * e, 0.0).astype(BF16)
            ks = jnp.where(second, 0.0, key * e).astype(BF16)
            a = lax.dot_general(qs, ks, (((1,), (1,)), ((), ())), preferred_element_type=F32)
            att = att + jnp.where((row // lv) == (col // lv), a, 0.0)

        diag = []
        for a0 in range(0, c, HGRN_SUB):
            qa, ka, ba = q[a0:a0 + HGRN_SUB], key[a0:a0 + HGRN_SUB], b[a0:a0 + HGRN_SUB]
            acc = jnp.zeros((HGRN_SUB, c), F32)
            for s in range(HGRN_SUB):
                e = jnp.exp(jnp.minimum(ba - ba[s:s + 1, :], 0.0))
                w = jnp.sum(qa * e * ka[s:s + 1, :], axis=1, keepdims=True)
                acc = jnp.where((sub_col == a0 + s) & (sub_row >= s), w, acc)
            diag.append(acc)
        att = att + jnp.concatenate(diag, axis=0)

        st = st_ref[...]
        b_last = b[c - 1:c, :]
        q_in = (q * jnp.exp(b)).astype(BF16)
        k_out = (key * jnp.exp(b_last - b)).astype(BF16)
        o = jnp.dot(att.astype(BF16), val, preferred_element_type=F32)
        o = o + lax.dot_general(q_in, st.astype(BF16), (((1,), (1,)), ((), ())),
                                preferred_element_type=F32)
        st_ref[...] = jnp.exp(b_last) * st + lax.dot_general(
            val, k_out, (((0,), (0,)), ((), ())), preferred_element_type=F32)

        o = o * lax.rsqrt(jnp.mean(o * o, axis=1, keepdims=True) + NORM_EPS)
        g = g_ref[rows, :]
        o_ref[rows, :] = (o * gain * (g * _sigmoid(g))).astype(o_ref.dtype)


def _hgrn2(q, f, i_in, g, lb, gain, batch, seq, nchunk=4):
    width = q.shape[1]
    heads = width // B_KEY_DIM
    tb = HGRN_CHUNK * nchunk
    q3, f3, i3, g3 = (t.reshape(batch, seq, width) for t in (q, f, i_in, g))
    blk = pl.BlockSpec((None, tb, B_KEY_DIM), lambda b, h, t: (b, t, h))
    vec = pl.BlockSpec((1, B_KEY_DIM), lambda b, h, t: (0, h))
    out = pl.pallas_call(
        functools.partial(_hgrn_kernel, nchunk=nchunk),
        out_shape=jax.ShapeDtypeStruct((batch, seq, width), BF16),
        grid=(batch, heads, seq // tb),
        in_specs=[blk, blk, blk, blk, vec, vec],
        out_specs=blk,
        scratch_shapes=[pltpu.VMEM((B_KEY_DIM, B_KEY_DIM), F32)],
        compiler_params=_cparams("parallel", "parallel", "arbitrary"),
        name="hgrn2",
    )(q3, f3, i3, g3, lb.reshape(1, width).astype(F32), gain.reshape(1, width).astype(F32))
    return out.reshape(batch * seq, width)


def _retention_kernel(q_ref, k_ref, v_ref, g_ref, gain_ref, o_ref, st_ref, *, nchunk):
    c = ATTN_BLOCK

    @pl.when(pl.program_id(1) == 0)
    def _():
        st_ref[...] = jnp.zeros_like(st_ref)

    ri = lax.broadcasted_iota(jnp.int32, (c, c), 0)
    ci_ = lax.broadcasted_iota(jnp.int32, (c, c), 1)
    rel = (ri - ci_).astype(F32)
    rowf = ri.astype(F32)

    for h in range(C_HEADS):
        log_gamma = math.log1p(-(2.0 ** (-5.0 - h)))
        decay = jnp.where(rel >= 0.0, jnp.exp(log_gamma * jnp.maximum(rel, 0.0)), 0.0)
        head_scale = jnp.exp(log_gamma * (rowf + 1.0))
        tail_scale = jnp.exp(log_gamma * (c - 1.0 - rowf))
        chunk_decay = math.exp(log_gamma * c)
        gain = gain_ref[:, h * C_VAL_DIM:(h + 1) * C_VAL_DIM]
        for ci in range(nchunk):
            rows = pl.ds(ci * c, c)
            q = q_ref[rows, h * C_KEY_DIM:(h + 1) * C_KEY_DIM].astype(BF16)
            k = k_ref[rows, h * C_KEY_DIM:(h + 1) * C_KEY_DIM].astype(BF16)
            v = v_ref[rows, h * C_VAL_DIM:(h + 1) * C_VAL_DIM]
            st = st_ref[h]
            s = lax.dot_general(q, k, (((1,), (1,)), ((), ())), preferred_element_type=F32)
            o = jnp.dot((s * decay).astype(BF16), v.astype(BF16), preferred_element_type=F32)
            o = o + head_scale * jnp.dot(q, st.astype(BF16), preferred_element_type=F32)
            st_ref[h] = chunk_decay * st + lax.dot_general(
                k, (v * tail_scale).astype(BF16), (((0,), (0,)), ((), ())),
                preferred_element_type=F32)
            o = o - jnp.mean(o, axis=1, keepdims=True)
            o = o * lax.rsqrt(jnp.mean(o * o, axis=1, keepdims=True) + NORM_EPS)
            g = g_ref[rows, h * C_VAL_DIM:(h + 1) * C_VAL_DIM]
            o_ref[rows, h * C_VAL_DIM:(h + 1) * C_VAL_DIM] = (
                o * gain * (g * _sigmoid(g))).astype(o_ref.dtype)


def _retention(q, k, v, g, gain, batch, seq, nchunk=4):
    tb = ATTN_BLOCK * nchunk
    kw, vw = q.shape[1], v.shape[1]
    q3, k3 = q.reshape(batch, seq, kw), k.reshape(batch, seq, kw)
    v3, g3 = v.reshape(batch, seq, vw), g.reshape(batch, seq, vw)
    kspec = pl.BlockSpec((None, tb, kw), lambda b, t: (b, t, 0))
    vspec = pl.BlockSpec((None, tb, vw), lambda b, t: (b, t, 0))
    out = pl.pallas_call(
        functools.partial(_retention_kernel, nchunk=nchunk),
        out_shape=jax.ShapeDtypeStruct((batch, seq, vw), BF16),
        grid=(batch, seq // tb),
        in_specs=[kspec, kspec, vspec, vspec, pl.BlockSpec((1, vw), lambda b, t: (0, 0))],
        out_specs=vspec,
        scratch_shapes=[pltpu.VMEM((C_HEADS, C_KEY_DIM, C_VAL_DIM), F32)],
        compiler_params=_cparams("parallel", "arbitrary"),
        name="retention",
    )(q3, k3, v3, g3, gain.reshape(1, vw).astype(F32))
    return out.reshape(batch * seq, vw)


def _layer_norm(z, g, b):
    mu = jnp.mean(z, axis=1, keepdims=True)
    zc = z - mu
    var = jnp.mean(zc * zc, axis=1, keepdims=True)
    return zc * lax.rsqrt(var + LN_EPS) * g + b


def _outproj_kernel(ya_ref, yb_ref, w_ref, x_ref, g_ref, b_ref, rwh_ref, rwl_ref, rb_ref,
                    x1_ref, idx_ref, gate_ref, *, alpha):
    half = ya_ref.shape[1]
    mix = jnp.dot(ya_ref[...], w_ref[0:half, :], preferred_element_type=F32)
    mix = mix + jnp.dot(yb_ref[...], w_ref[half:, :], preferred_element_type=F32)
    x1 = _layer_norm(alpha * x_ref[...] + mix, g_ref[...], b_ref[...])
    _to_tiles(x1_ref, x1)

    xh = x1.astype(BF16)
    xl = (x1 - xh.astype(F32)).astype(BF16)
    logits = (jnp.dot(xh, rwh_ref[...], preferred_element_type=F32)
              + jnp.dot(xh, rwl_ref[...], preferred_element_type=F32)
              + jnp.dot(xl, rwh_ref[...], preferred_element_type=F32)) + rb_ref[...]
    lane = lax.broadcasted_iota(jnp.int32, logits.shape, 1)
    idx_acc = jnp.zeros(logits.shape, jnp.int32)
    val_acc = jnp.zeros(logits.shape, F32)
    top0 = None
    den = None
    for kk in range(TOP_K):
        m = jnp.max(logits, axis=1, keepdims=True)
        sel = jnp.min(jnp.where(logits == m, lane, LANES), axis=1, keepdims=True)
        if kk == 0:
            top0 = m
        e = jnp.exp(m - top0)
        den = e if den is None else den + e
        idx_acc = jnp.where(lane == kk, sel, idx_acc)
        val_acc = jnp.where(lane == kk, e, val_acc)
        logits = jnp.where(lane == sel, NEG_BIG * 2.0, logits)
    idx_ref[...] = idx_acc[:, 0:idx_ref.shape[1]]
    gate_ref[...] = (val_acc / den)[:, 0:gate_ref.shape[1]]


def _outproj_norm_route(ya, yb, w_out, x, ln_g, ln_b, rw, rb, alpha, tm=256):
    n, d = x.shape
    half = ya.shape[1]
    rw_pad = jnp.zeros((d, LANES), F32).at[:, :N_EXPERTS].set(rw.astype(F32))
    rwh = rw_pad.astype(BF16)
    rwl = (rw_pad - rwh.astype(F32)).astype(BF16)
    rb_pad = jnp.full((1, LANES), NEG_BIG, F32).at[0, :N_EXPERTS].set(rb.astype(F32))
    row = lambda i: (i, 0)
    fixed = lambda i: (0, 0)
    return pl.pallas_call(
        functools.partial(_outproj_kernel, alpha=alpha),
        out_shape=(jax.ShapeDtypeStruct((n * SUBLANES, LANES), F32),
                   jax.ShapeDtypeStruct((n, 8), jnp.int32),
                   jax.ShapeDtypeStruct((n, 8), F32)),
        grid=(n // tm,),
        in_specs=[
            pl.BlockSpec((tm, half), row),
            pl.BlockSpec((tm, half), row),
            pl.BlockSpec((2 * half, d), fixed),
            pl.BlockSpec((tm, d), row),
            pl.BlockSpec((1, d), fixed),
            pl.BlockSpec((1, d), fixed),
            pl.BlockSpec((d, LANES), fixed),
            pl.BlockSpec((d, LANES), fixed),
            pl.BlockSpec((1, LANES), fixed),
        ],
        out_specs=(pl.BlockSpec((tm * SUBLANES, LANES), row),
                   pl.BlockSpec((tm, 8), row),
                   pl.BlockSpec((tm, 8), row)),
        compiler_params=_cparams("parallel"),
        name="outproj_norm_route",
    )(ya, yb, w_out.astype(BF16), x, ln_g.reshape(1, d), ln_b.reshape(1, d), rwh, rwl, rb_pad)


def _gather_tiles(idx_hbm, src_hbm, dst_ref, idx_smem, idx_sem, row_sem, count):
    i = pl.program_id(0)
    slot = i % 2

    def idx_copy(step, s):
        return pltpu.make_async_copy(idx_hbm.at[step], idx_smem.at[s], idx_sem.at[s])

    @pl.when(i == 0)
    def _():
        idx_copy(0, 0).start()

    idx_copy(i, slot).wait()

    @pl.when(i + 1 < pl.num_programs(0))
    def _():
        idx_copy(i + 1, 1 - slot).start()

    def row_copy(r):
        src = pl.multiple_of(idx_smem[slot, r] * SUBLANES, SUBLANES)
        dst = pl.multiple_of(r * SUBLANES, SUBLANES)
        return pltpu.make_async_copy(src_hbm.at[pl.ds(src, SUBLANES), :],
                                     dst_ref.at[pl.ds(dst, SUBLANES), :], row_sem)

    def issue(r, carry):
        row_copy(r).start()
        return carry

    def drain(r, carry):
        row_copy(r).wait()
        return carry

    lax.fori_loop(0, count, issue, 0, unroll=8)
    lax.fori_loop(0, count, drain, 0, unroll=16)


def _gather_rows_kernel(idx_hbm, src_hbm, out_ref, idx_smem, idx_sem, row_sem, *, rows):
    _gather_tiles(idx_hbm, src_hbm, out_ref, idx_smem, idx_sem, row_sem, rows)


def _gather_rows(src, idx, rows=MOE_BLOCK):
    nblocks = idx.shape[0]
    return pl.pallas_call(
        functools.partial(_gather_rows_kernel, rows=rows),
        out_shape=jax.ShapeDtypeStruct((nblocks * rows * SUBLANES, LANES), src.dtype),
        grid=(nblocks,),
        in_specs=[pl.BlockSpec(memory_space=pl.ANY), pl.BlockSpec(memory_space=pl.ANY)],
        out_specs=pl.BlockSpec((rows * SUBLANES, LANES), lambda i: (i, 0)),
        scratch_shapes=[pltpu.SMEM((2, rows), jnp.int32),
                        pltpu.SemaphoreType.DMA((2,)),
                        pltpu.SemaphoreType.DMA(())],
        compiler_params=_cparams("arbitrary"),
        name="gather_rows",
    )(idx, src)


def _moe_kernel(be_ref, nv_ref, xs_ref, wgu_ref, bgu_ref, wdn_ref, bdn_ref, y_ref, act_ref,
                wgu_bf, wdn_bf, *, chunk):
    i = pl.program_id(0)
    de = wdn_ref.shape[0]

    @pl.when((i == 0) | (be_ref[i] != be_ref[jnp.maximum(i - 1, 0)]))
    def _():
        for c0 in range(0, 2 * de, chunk):
            wgu_bf[:, c0:c0 + chunk] = wgu_ref[:, c0:c0 + chunk].astype(BF16)
        for c0 in range(0, wdn_ref.shape[1], chunk):
            wdn_bf[:, c0:c0 + chunk] = wdn_ref[:, c0:c0 + chunk].astype(BF16)

    @pl.when(nv_ref[i] > 0)
    def _():
        xb = _from_tiles(xs_ref, MOE_BLOCK).astype(BF16)
        for c0 in range(0, de, chunk):
            gate = jnp.dot(xb, wgu_bf[:, c0:c0 + chunk], preferred_element_type=F32)
            gate = gate + bgu_ref[:, c0:c0 + chunk]
            up = jnp.dot(xb, wgu_bf[:, de + c0:de + c0 + chunk], preferred_element_type=F32)
            up = up + bgu_ref[:, de + c0:de + c0 + chunk]
            gate = jnp.minimum(gate, SWIGLU_LIMIT)
            up = jnp.clip(up, -SWIGLU_LIMIT, SWIGLU_LIMIT)
            act = (up + 1.0) * gate * _sigmoid(SWIGLU_ALPHA * gate)
            act_ref[:, c0:c0 + chunk] = act.astype(BF16)
        _to_tiles(y_ref, jnp.dot(act_ref[...], wdn_bf[...], preferred_element_type=F32)
                  + bdn_ref[...])

    @pl.when(nv_ref[i] == 0)
    def _():
        y_ref[...] = jnp.zeros_like(y_ref)


def _moe_experts(xs, block_e, nvalid, w_gu, b_gu, w_dn, b_dn, layer, chunk=512):
    n_exp, de, d = w_dn.shape[1:]
    assert d == SUBLANES * LANES
    cap = xs.shape[0] // SUBLANES
    nblocks = cap // MOE_BLOCK
    tile_rows = MOE_BLOCK * SUBLANES
    grid_spec = pltpu.PrefetchScalarGridSpec(
        num_scalar_prefetch=2,
        grid=(nblocks,),
        in_specs=[
            pl.BlockSpec((tile_rows, LANES), lambda i, be, nv: (i, 0)),
            pl.BlockSpec((None, None, d, 2 * de), lambda i, be, nv: (layer, be[i], 0, 0)),
            pl.BlockSpec((None, None, 1, 2 * de), lambda i, be, nv: (layer, be[i], 0, 0)),
            pl.BlockSpec((None, None, de, d), lambda i, be, nv: (layer, be[i], 0, 0)),
            pl.BlockSpec((None, None, 1, d), lambda i, be, nv: (layer, be[i], 0, 0)),
        ],
        out_specs=pl.BlockSpec((tile_rows, LANES), lambda i, be, nv: (i, 0)),
        scratch_shapes=[pltpu.VMEM((MOE_BLOCK, de), BF16),
                        pltpu.VMEM((d, 2 * de), BF16),
                        pltpu.VMEM((de, d), BF16)],
    )
    depth = w_dn.shape[0]
    return pl.pallas_call(
        functools.partial(_moe_kernel, chunk=chunk),
        out_shape=jax.ShapeDtypeStruct((cap * SUBLANES, LANES), F32),
        grid_spec=grid_spec,
        compiler_params=pltpu.CompilerParams(dimension_semantics=("arbitrary",),
                                             vmem_limit_bytes=MOE_VMEM_LIMIT),
        name="moe_experts",
    )(block_e, nvalid, xs, w_gu, b_gu.reshape(depth, n_exp, 1, 2 * de),
      w_dn, b_dn.reshape(depth, n_exp, 1, d))


def _combine_kernel(pos_hbm, y_hbm, x_ref, gate_ref, g_ref, b_ref, o_ref,
                    buf_ref, pos_smem, pos_sem, row_sem, *, tm, alpha):
    _gather_tiles(pos_hbm, y_hbm, buf_ref, pos_smem, pos_sem, row_sem, TOP_K * tm)
    gates = gate_ref[...]
    z = alpha * _from_tiles(x_ref, tm)
    for kk in range(TOP_K):
        z = z + gates[:, kk:kk + 1] * _from_tiles(buf_ref, tm, base=kk * tm * SUBLANES)
    o_ref[...] = _layer_norm(z, g_ref[...], b_ref[...])


def _combine_norm(y, pos, x1, gates, ln_g, ln_b, alpha, tm=256):
    d = SUBLANES * LANES
    n = x1.shape[0] // SUBLANES
    steps = n // tm
    pos_t = pos.reshape(steps, tm, TOP_K).transpose(0, 2, 1).reshape(steps, TOP_K * tm)
    row = lambda i: (i, 0)
    fixed = lambda i: (0, 0)
    return pl.pallas_call(
        functools.partial(_combine_kernel, tm=tm, alpha=alpha),
        out_shape=jax.ShapeDtypeStruct((n, d), F32),
        grid=(steps,),
        in_specs=[
            pl.BlockSpec(memory_space=pl.ANY),
            pl.BlockSpec(memory_space=pl.ANY),
            pl.BlockSpec((tm * SUBLANES, LANES), row),
            pl.BlockSpec((tm, 8), row),
            pl.BlockSpec((1, d), fixed),
            pl.BlockSpec((1, d), fixed),
        ],
        out_specs=pl.BlockSpec((tm, d), row),
        scratch_shapes=[pltpu.VMEM((TOP_K * tm * SUBLANES, LANES), F32),
                        pltpu.SMEM((2, TOP_K * tm), jnp.int32),
                        pltpu.SemaphoreType.DMA((2,)),
                        pltpu.SemaphoreType.DMA(())],
        compiler_params=_cparams("arbitrary"),
        name="combine_norm",
    )(pos_t, y, x1, gates, ln_g.reshape(1, d), ln_b.reshape(1, d))


def _routing_tables(top_idx, n):
    nk = n * TOP_K
    flat_e = top_idx[:, :TOP_K].reshape(nk)
    assign = jnp.arange(nk, dtype=jnp.int32)
    skey = lax.sort(flat_e * nk + assign)
    order = skey % nk
    bounds = jnp.searchsorted(skey // nk, jnp.arange(N_EXPERTS + 1, dtype=jnp.int32),
                              side="left").astype(jnp.int32)
    start, counts = bounds[:-1], bounds[1:] - bounds[:-1]
    _, sorted_at = lax.sort((order, assign), num_keys=1)
    padded = (counts + MOE_BLOCK - 1) // MOE_BLOCK * MOE_BLOCK
    pad_end = jnp.cumsum(padded)
    pad_start = pad_end - padded
    pos = (pad_start[flat_e] + sorted_at - start[flat_e]).astype(jnp.int32)
    n_blocks = nk // MOE_BLOCK + N_EXPERTS + 1
    blk_first = jnp.arange(n_blocks, dtype=jnp.int32) * MOE_BLOCK
    block_e = jnp.minimum(jnp.searchsorted(pad_end, blk_first, side="right"),
                          N_EXPERTS - 1).astype(jnp.int32)
    nvalid = jnp.clip(counts[block_e] - (blk_first - pad_start[block_e]), 0, MOE_BLOCK)
    nvalid = jnp.where(blk_first < pad_end[-1], nvalid, 0).astype(jnp.int32)
    slot = jnp.arange(n_blocks * MOE_BLOCK, dtype=jnp.int32)
    slot_e = jnp.repeat(block_e, MOE_BLOCK)
    within = slot - pad_start[slot_e]
    live = (within < counts[slot_e]) & (slot < pad_end[-1])
    src = order[jnp.clip(start[slot_e] + within, 0, nk - 1)] // TOP_K
    rows = jnp.where(live, src, 0).astype(jnp.int32)
    return rows.reshape(n_blocks, MOE_BLOCK), pos.reshape(n, TOP_K), block_e, nvalid


_EVEN_PLAN = (
    (512, "attn", HEAD_DIM ** -0.5), (128, "attn", 1.0), (128, None, 1.0),
    (512, None, 1.0), (512, None, 1.0), (512, None, 1.0), (512, None, 1.0),
)
_ODD_PLAN = (
    (256, "ret", 1.0), (256, "ret", C_KEY_DIM ** -0.5), (512, None, 1.0), (512, None, 1.0),
) + ((512, "attn", HEAD_DIM ** -0.5), (128, "attn", 1.0), (128, None, 1.0)) * len(D_PATTERNS)


def kernel(x, w_in_even, b_in_even, attn_sinks, hgrn_lb_logits, hgrn_norm, w_out_even, w_in_odd,
           b_in_odd, ret_norm, w_out_odd, ln1_g, ln1_b, ln2_g, ln2_b, router_w, router_b,
           expert_w_gu, expert_b_gu, expert_w_dn, expert_b_dn):
    batch, seq, d = x.shape
    n = batch * seq
    depth = ln1_g.shape[0]
    alpha = float((2 * depth) ** 0.25)

    rope_inv = 1.0 / (ROPE_THETA ** (jnp.arange(0, ROPE_DIM, 2, dtype=F32) / ROPE_DIM))
    ret_inv = 1.0 / (RET_THETA ** jnp.linspace(0.0, 1.0, C_KEY_DIM // 2, dtype=F32))
    tab_attn = _rope_tables(seq, ROPE_DIM // 2, rope_inv, HEAD_DIM)
    tab_ret = _rope_tables(seq, C_KEY_DIM // 2, ret_inv, C_KEY_DIM)
    lb_soft = jax.nn.softmax(hgrn_lb_logits.astype(F32), axis=0)
    lower_bounds = jnp.concatenate(
        [jnp.zeros_like(lb_soft[:1]), jnp.cumsum(lb_soft, axis=0)[:-1]], axis=0)

    xf = x.reshape(n, d)
    for layer in range(depth):
        j = layer // 2
        if layer % 2 == 0:
            aq, ak, av, bq, bf, bi, bg = _inproj(
                xf, w_in_even[j].astype(BF16), b_in_even[j].reshape(1, -1), tab_attn, tab_ret,
                _EVEN_PLAN, seq)
            (ya,) = _banded_attention(aq, ak, av, batch, seq, 1, A_WINDOW - 1,
                                      sinks=attn_sinks[j])
            yb = _hgrn2(bq, bf, bi, bg, lower_bounds[j], hgrn_norm[j], batch, seq)
            w_out = w_out_even[j]
        else:
            outs = _inproj(xf, w_in_odd[j].astype(BF16), b_in_odd[j].reshape(1, -1), tab_attn,
                           tab_ret, _ODD_PLAN, seq)
            cq, ck, cv, cg = outs[:4]
            ya = _retention(cq, ck, cv, cg, ret_norm[j], batch, seq)
            parts = []
            for p, (window, dil) in enumerate(D_PATTERNS):
                dq, dk, dv = outs[4 + 3 * p:7 + 3 * p]
                o, lse = _banded_attention(dq, dk, dv, batch, seq, dil, window // dil,
                                           want_lse=True, out_dtype=F32)
                parts += [o, lse]
            yb = _merge_patterns(parts)
            w_out = w_out_odd[j]
        x1, top_idx, gates = _outproj_norm_route(
            ya, yb, w_out, xf, ln1_g[layer], ln1_b[layer], router_w[layer], router_b[layer], alpha)
        rows, pos, block_e, nvalid = _routing_tables(top_idx, n)
        xs = _gather_rows(x1, rows)
        y = _moe_experts(xs, block_e, nvalid, expert_w_gu, expert_b_gu, expert_w_dn, expert_b_dn,
                         layer)
        xf = _combine_norm(y, pos, x1, gates, ln2_g[layer], ln2_b[layer], alpha)
    return xf.reshape(batch, seq, d)
```

```python
import functools
import math

import jax
import jax.numpy as jnp
from jax import lax
from jax.experimental import pallas as pl
from jax.experimental.pallas import tpu as pltpu

F32 = jnp.float32
BF16 = jnp.bfloat16

HEAD_DIM = 64
ATTN_BLOCK = 128
A_Q_HEADS = 8
A_KV_HEADS = 2
A_WINDOW = 128
B_HEADS = 4
B_KEY_DIM = 128
B_MIN_F = 1e-30
C_HEADS = 4
C_KEY_DIM = 64
C_VAL_DIM = 128
D_PATTERNS = ((128, 1), (512, 4), (2048, 16))
ROPE_THETA = 500000.0
ROPE_DIM = HEAD_DIM // 4
RET_THETA = 10000.0
N_EXPERTS = 32
TOP_K = 4
SWIGLU_LIMIT = 7.0
SWIGLU_ALPHA = 1.702
MOE_BLOCK = 512
LN_EPS = 1e-5
NORM_EPS = 1e-6
NEG_BIG = -1e30

LANES = 128
SUBLANES = 8
VMEM_LIMIT = 48 * 1024 * 1024
MOE_VMEM_LIMIT = 56 * 1024 * 1024

HGRN_CHUNK = 128
HGRN_SUB = 8
HGRN_LEVELS = (16, 32, 64, 128)


def _cparams(*sem):
    return pltpu.CompilerParams(dimension_semantics=sem, vmem_limit_bytes=VMEM_LIMIT)


def _to_tiles(ref, value):
    rows = value.shape[0]
    for c in range(SUBLANES):
        ref[pl.ds(c, rows, stride=SUBLANES), :] = value[:, c * LANES:(c + 1) * LANES]


def _from_tiles(ref, rows, base=0):
    return jnp.concatenate(
        [ref[pl.ds(base + c, rows, stride=SUBLANES), :] for c in range(SUBLANES)], axis=1)


def _sigmoid(z):
    return 1.0 / (1.0 + jnp.exp(-z))


def _rotate(h, cos, sa, sb, shift):
    width = h.shape[1]
    rep = width // LANES
    if rep > 1:
        cos = jnp.concatenate([cos] * rep, axis=1)
        sa = jnp.concatenate([sa] * rep, axis=1)
        sb = jnp.concatenate([sb] * rep, axis=1)
    up = pltpu.roll(h, width - shift, axis=1)
    dn = pltpu.roll(h, shift, axis=1)
    return h * cos + up * sa + dn * sb


def _inproj_kernel(x_ref, w_ref, b_ref, ta_ref, tr_ref, *out_refs, plan):
    xb = x_ref[...].astype(BF16)
    off = 0
    for o_ref, (width, rope, scale) in zip(out_refs, plan):
        h = jnp.dot(xb, w_ref[:, off:off + width], preferred_element_type=F32)
        h = h + b_ref[:, off:off + width]
        if rope == "attn":
            h = _rotate(h, ta_ref[0], ta_ref[1], ta_ref[2], ROPE_DIM // 2)
        elif rope == "ret":
            h = _rotate(h, tr_ref[0], tr_ref[1], tr_ref[2], C_KEY_DIM // 2)
        if scale != 1.0:
            h = h * scale
        o_ref[...] = h.astype(o_ref.dtype)
        off += width


def _inproj(x, w, b, tab_attn, tab_ret, plan, seq, tm=512):
    n, d = x.shape
    cols = w.shape[1]
    tpb = seq // tm
    kern = functools.partial(_inproj_kernel, plan=plan)
    outs = tuple(jax.ShapeDtypeStruct((n, wd), F32) for wd, _, _ in plan)
    return pl.pallas_call(
        kern,
        out_shape=outs,
        grid=(n // tm,),
        in_specs=[
            pl.BlockSpec((tm, d), lambda i: (i, 0)),
            pl.BlockSpec((d, cols), lambda i: (0, 0)),
            pl.BlockSpec((1, cols), lambda i: (0, 0)),
            pl.BlockSpec((3, tm, LANES), lambda i: (0, i % tpb, 0)),
            pl.BlockSpec((3, tm, LANES), lambda i: (0, i % tpb, 0)),
        ],
        out_specs=tuple(pl.BlockSpec((tm, wd), lambda i: (i, 0)) for wd, _, _ in plan),
        compiler_params=_cparams("parallel"),
        name="inproj",
    )(x, w, b, tab_attn, tab_ret)


def _rope_tables(seq, half, inv_freq, period):
    pos = jnp.arange(seq, dtype=F32)
    ang = pos[:, None] * inv_freq[None, :]
    cos, sin = jnp.cos(ang), jnp.sin(ang)
    pad = period - 2 * half
    ones = jnp.ones((seq, pad), F32)
    zeros = jnp.zeros((seq, pad), F32)
    zh = jnp.zeros((seq, half), F32)
    cosf = jnp.concatenate([cos, cos, ones], axis=1)
    sa = jnp.concatenate([-sin, zh, zeros], axis=1)
    sb = jnp.concatenate([zh, sin, zeros], axis=1)
    rep = LANES // period
    return jnp.stack([jnp.tile(t, (1, rep)) for t in (cosf, sa, sb)], axis=0)


def _attn_kernel(*refs, dil, max_dist, nsub, has_sink, want_lse):
    if has_sink:
        sink_ref, refs = refs[0], refs[1:]
    q_refs = refs[:4]
    k_ref, v_ref, kp_ref, vp_ref = refs[4:8]
    o_ref = refs[8]
    lse_ref = refs[9] if want_lse else None
    n_out = 2 if want_lse else 1
    stage = refs[8 + n_out:]
    blk = ATTN_BLOCK
    qi = lax.broadcasted_iota(jnp.int32, (4 * blk, 2 * blk), 0) % blk
    kj = lax.broadcasted_iota(jnp.int32, (4 * blk, 2 * blk), 1)
    dist = qi - kj + blk
    band = (dist >= 0) & (dist <= max_dist)
    first_kmin = jnp.where(pl.program_id(1) == 0, blk, 0)
    band_first = band & (kj >= first_kmin)
    row4 = lax.broadcasted_iota(jnp.int32, (4 * blk, 1), 0) // blk

    def take(start):
        return pl.ds(start, blk, stride=dil) if dil > 1 else pl.ds(start, blk)

    def one_group(r, j):
        rows = take(j * blk * dil + r)
        if j == 0:
            k_prev, v_prev = kp_ref[take(r), :], vp_ref[take(r), :]
            valid = band_first
        else:
            prow = take((j - 1) * blk * dil + r)
            k_prev, v_prev = k_ref[prow, :], v_ref[prow, :]
            valid = band
        k_cat = jnp.concatenate([k_prev, k_ref[rows, :]], axis=0).astype(BF16)
        v_cat = jnp.concatenate([v_prev, v_ref[rows, :]], axis=0).astype(BF16)
        for h in range(A_KV_HEADS):
            qa, qb = q_refs[2 * h][rows, :], q_refs[2 * h + 1][rows, :]
            q4 = jnp.concatenate([qa[:, :HEAD_DIM], qa[:, HEAD_DIM:], qb[:, :HEAD_DIM],
                                  qb[:, HEAD_DIM:]], axis=0).astype(BF16)
            kh = k_cat[:, h * HEAD_DIM:(h + 1) * HEAD_DIM]
            vh = v_cat[:, h * HEAD_DIM:(h + 1) * HEAD_DIM]
            s = lax.dot_general(q4, kh, (((1,), (1,)), ((), ())), preferred_element_type=F32)
            s = jnp.where(valid, s, NEG_BIG)
            m = jnp.max(s, axis=1, keepdims=True)
            if has_sink:
                sk = jnp.full((4 * blk, 1), sink_ref[4 * h + 3], F32)
                for g in range(3):
                    sk = jnp.where(row4 == g, sink_ref[4 * h + g], sk)
                m = jnp.maximum(m, sk)
            p = jnp.exp(s - m)
            den = jnp.sum(p, axis=1, keepdims=True)
            if has_sink:
                den = den + jnp.exp(sk - m)
            o = jnp.dot(p.astype(BF16), vh, preferred_element_type=F32) / den
            lse = m + jnp.log(den) if want_lse else None
            for half in range(2):
                cb = 2 * h + half
                g0 = 2 * half
                pair = jnp.concatenate(
                    [o[g0 * blk:(g0 + 1) * blk], o[(g0 + 1) * blk:(g0 + 2) * blk]], axis=1)
                if dil > 1:
                    stage[0][cb, rows, :] = pair
                else:
                    o_ref[rows, cb * LANES:(cb + 1) * LANES] = pair.astype(o_ref.dtype)
                if want_lse:
                    lpair = jnp.concatenate(
                        [jnp.broadcast_to(lse[g * blk:(g + 1) * blk], (blk, HEAD_DIM))
                         for g in (g0, g0 + 1)], axis=1)
                    if dil > 1:
                        stage[1][cb, rows, :] = lpair
                    else:
                        lse_ref[rows, cb * LANES:(cb + 1) * LANES] = lpair

    for j in range(nsub):
        if dil == 1:
            one_group(0, j)
        else:
            def body(r, carry, j=j):
                one_group(r, j)
                return carry
            lax.fori_loop(0, dil, body, 0)
    if dil > 1:
        for cb in range(4):
            o_ref[:, cb * LANES:(cb + 1) * LANES] = stage[0][cb].astype(o_ref.dtype)
            if want_lse:
                lse_ref[:, cb * LANES:(cb + 1) * LANES] = stage[1][cb]


def _banded_attention(q, k, v, batch, seq, dil, max_dist, sinks=None, want_lse=False,
                      out_dtype=BF16):
    span = ATTN_BLOCK * dil
    nsub = max(1, 512 // span)
    tb = span * nsub
    nt = seq // tb
    qw, kw = q.shape[1], k.shape[1]
    q3, k3, v3 = (t.reshape(batch, seq, t.shape[1]) for t in (q, k, v))
    kern = functools.partial(_attn_kernel, dil=dil, max_dist=max_dist, nsub=nsub,
                             has_sink=sinks is not None, want_lse=want_lse)
    cur = lambda b, t: (b, t, 0)
    prev = lambda b, t: (b, jnp.maximum(t * nsub - 1, 0), 0)
    in_specs = [pl.BlockSpec((None, tb, LANES), lambda b, t, c=c: (b, t, c)) for c in range(4)] + [
        pl.BlockSpec((None, tb, kw), cur),
        pl.BlockSpec((None, tb, kw), cur),
        pl.BlockSpec((None, span, kw), prev),
        pl.BlockSpec((None, span, kw), prev),
    ]
    args = [q3, q3, q3, q3, k3, v3, k3, v3]
    if sinks is not None:
        in_specs = [pl.BlockSpec(memory_space=pltpu.SMEM)] + in_specs
        args = [sinks.astype(F32)] + args
    n_out = 2 if want_lse else 1
    out_shape = [jax.ShapeDtypeStruct((batch, seq, qw), out_dtype)]
    out_specs = [pl.BlockSpec((None, tb, qw), cur)]
    if want_lse:
        out_shape.append(jax.ShapeDtypeStruct((batch, seq, qw), F32))
        out_specs.append(pl.BlockSpec((None, tb, qw), cur))
    scratch = [pltpu.VMEM((4, tb, LANES), F32)] * n_out if dil > 1 else []
    res = pl.pallas_call(
        kern,
        out_shape=tuple(out_shape),
        grid=(batch, nt),
        in_specs=in_specs,
        out_specs=tuple(out_specs),
        scratch_shapes=scratch,
        compiler_params=_cparams("parallel", "arbitrary"),
        name=f"banded_attn_d{dil}",
    )(*args)
    return tuple(r.reshape(batch * seq, qw) for r in res)


def _merge_kernel(o1, l1, o2, l2, o3, l3, out_ref):
    a, b, c = l1[...], l2[...], l3[...]
    m = jnp.maximum(jnp.maximum(a, b), c)
    ea, eb, ec = jnp.exp(a - m), jnp.exp(b - m), jnp.exp(c - m)
    num = ea * o1[...] + eb * o2[...] + ec * o3[...]
    out_ref[...] = (num / (ea + eb + ec)).astype(out_ref.dtype)


def _merge_patterns(parts, tm=512):
    n, w = parts[0].shape
    spec = pl.BlockSpec((tm, w), lambda i: (i, 0))
    return pl.pallas_call(
        _merge_kernel,
        out_shape=jax.ShapeDtypeStruct((n, w), BF16),
        grid=(n // tm,),
        in_specs=[spec] * 6,
        out_specs=spec,
        compiler_params=_cparams("parallel"),
        name="merge_patterns",
    )(*parts)


def _hgrn_structure():
    c = HGRN_CHUNK
    r = lax.broadcasted_iota(jnp.int32, (c, c), 0)
    j = lax.broadcasted_iota(jnp.int32, (c, c), 1)
    blocks = [(j <= r)]
    for lv in HGRN_LEVELS:
        mid = (r // lv) * lv + lv // 2 - 1
        second = (r % lv) >= lv // 2
        lo = jnp.where(second, mid, r)
        hi = jnp.where(second, r, mid)
        blocks.append((j > lo) & (j <= hi))
    return jnp.concatenate([jnp.where(m, 1.0, 0.0) for m in blocks], axis=0).astype(BF16)


def _split3(x):
    hi = x.astype(BF16)
    r1 = x - hi.astype(F32)
    mid = r1.astype(BF16)
    lo = (r1 - mid.astype(F32)).astype(BF16)
    return hi, mid, lo


def _hgrn_kernel(q_ref, f_ref, i_ref, g_ref, lb_ref, gain_ref, o_ref, st_ref, *, nchunk):
    c = HGRN_CHUNK

    @pl.when(pl.program_id(2) == 0)
    def _():
        st_ref[...] = jnp.zeros_like(st_ref)

    dmat = _hgrn_structure()
    row = lax.broadcasted_iota(jnp.int32, (c, c), 0)
    col = lax.broadcasted_iota(jnp.int32, (c, c), 1)
    lb = jnp.clip(lb_ref[...], 0.0, 1.0)
    gain = gain_ref[...]
    sub_row = lax.broadcasted_iota(jnp.int32, (HGRN_SUB, c), 0)
    sub_col = lax.broadcasted_iota(jnp.int32, (HGRN_SUB, c), 1)

    for ci in range(nchunk):
        rows = pl.ds(ci * c, c)
        q = q_ref[rows, :]
        f = lb + (1.0 - lb) * _sigmoid(f_ref[rows, :])
        lf = jnp.log(jnp.maximum(f, B_MIN_F))
        key = 1.0 - f
        val = i_ref[rows, :].astype(BF16)

        hi, mid, lo = _split3(lf)
        sums = (jnp.dot(dmat, hi, preferred_element_type=F32)
                + jnp.dot(dmat, mid, preferred_element_type=F32)
                + jnp.dot(dmat, lo, preferred_element_type=F32))
        b = sums[0:c]

        att = jnp.zeros((c, c), F32)
        for n, lv in enumerate(HGRN_LEVELS):
            e = jnp.exp(sums[(n + 1) * c:(n + 2) * c])
            second = (row % lv) >= lv // 2
            qs = jnp.where(second, q * e, 0.0).astype(BF16)
            ks = jnp.where(second, 0.0, key * e).astype(BF16)
            a = lax.dot_general(qs, ks, (((1,), (1,)), ((), ())), preferred_element_type=F32)
            att = att + jnp.where((row // lv) == (col // lv), a, 0.0)

        diag = []
        for a0 in range(0, c, HGRN_SUB):
            qa, ka, ba = q[a0:a0 + HGRN_SUB], key[a0:a0 + HGRN_SUB], b[a0:a0 + HGRN_SUB]
            acc = jnp.zeros((HGRN_SUB, c), F32)
            for s in range(HGRN_SUB):
                e = jnp.exp(jnp.minimum(ba - ba[s:s + 1, :], 0.0))
                w = jnp.sum(qa * e * ka[s:s + 1, :], axis=1, keepdims=True)
                acc = jnp.where((sub_col == a0 + s) & (sub_row >= s), w, acc)
            diag.append(acc)
        att = att + jnp.concatenate(diag, axis=0)

        st = st_ref[...]
        b_last = b[c - 1:c, :]
        q_in = (q * jnp.exp(b)).astype(BF16)
        k_out = (key * jnp.exp(b_last - b)).astype(BF16)
        o = jnp.dot(att.astype(BF16), val, preferred_element_type=F32)
        o = o + lax.dot_general(q_in, st.astype(BF16), (((1,), (1,)), ((), ())),
                                preferred_element_type=F32)
        st_ref[...] = jnp.exp(b_last) * st + lax.dot_general(
            val, k_out, (((0,), (0,)), ((), ())), preferred_element_type=F32)

        o = o * lax.rsqrt(jnp.mean(o * o, axis=1, keepdims=True) + NORM_EPS)
        g = g_ref[rows, :]
        o_ref[rows, :] = (o * gain * (g * _sigmoid(g))).astype(o_ref.dtype)


def _hgrn2(q, f, i_in, g, lb, gain, batch, seq, nchunk=4):
    width = q.shape[1]
    heads = width // B_KEY_DIM
    tb = HGRN_CHUNK * nchunk
    q3, f3, i3, g3 = (t.reshape(batch, seq, width) for t in (q, f, i_in, g))
    blk = pl.BlockSpec((None, tb, B_KEY_DIM), lambda b, h, t: (b, t, h))
    vec = pl.BlockSpec((1, B_KEY_DIM), lambda b, h, t: (0, h))
    out = pl.pallas_call(
        functools.partial(_hgrn_kernel, nchunk=nchunk),
        out_shape=jax.ShapeDtypeStruct((batch, seq, width), BF16),
        grid=(batch, heads, seq // tb),
        in_specs=[blk, blk, blk, blk, vec, vec],
        out_specs=blk,
        scratch_shapes=[pltpu.VMEM((B_KEY_DIM, B_KEY_DIM), F32)],
        compiler_params=_cparams("parallel", "parallel", "arbitrary"),
        name="hgrn2",
    )(q3, f3, i3, g3, lb.reshape(1, width).astype(F32), gain.reshape(1, width).astype(F32))
    return out.reshape(batch * seq, width)


def _retention_kernel(q_ref, k_ref, v_ref, g_ref, gain_ref, o_ref, st_ref, *, nchunk):
    c = ATTN_BLOCK

    @pl.when(pl.program_id(1) == 0)
    def _():
        st_ref[...] = jnp.zeros_like(st_ref)

    ri = lax.broadcasted_iota(jnp.int32, (c, c), 0)
    ci_ = lax.broadcasted_iota(jnp.int32, (c, c), 1)
    rel = (ri - ci_).astype(F32)
    rowf = ri.astype(F32)

    for h in range(C_HEADS):
        log_gamma = math.log1p(-(2.0 ** (-5.0 - h)))
        decay = jnp.where(rel >= 0.0, jnp.exp(log_gamma * jnp.maximum(rel, 0.0)), 0.0)
        head_scale = jnp.exp(log_gamma * (rowf + 1.0))
        tail_scale = jnp.exp(log_gamma * (c - 1.0 - rowf))
        chunk_decay = math.exp(log_gamma * c)
        gain = gain_ref[:, h * C_VAL_DIM:(h + 1) * C_VAL_DIM]
        for ci in range(nchunk):
            rows = pl.ds(ci * c, c)
            q = q_ref[rows, h * C_KEY_DIM:(h + 1) * C_KEY_DIM].astype(BF16)
            k = k_ref[rows, h * C_KEY_DIM:(h + 1) * C_KEY_DIM].astype(BF16)
            v = v_ref[rows, h * C_VAL_DIM:(h + 1) * C_VAL_DIM]
            st = st_ref[h]
            s = lax.dot_general(q, k, (((1,), (1,)), ((), ())), preferred_element_type=F32)
            o = jnp.dot((s * decay).astype(BF16), v.astype(BF16), preferred_element_type=F32)
            o = o + head_scale * jnp.dot(q, st.astype(BF16), preferred_element_type=F32)
            st_ref[h] = chunk_decay * st + lax.dot_general(
                k, (v * tail_scale).astype(BF16), (((0,), (0,)), ((), ())),
                preferred_element_type=F32)
            o = o - jnp.mean(o, axis=1, keepdims=True)
            o = o * lax.rsqrt(jnp.mean(o * o, axis=1, keepdims=True) + NORM_EPS)
            g = g_ref[rows, h * C_VAL_DIM:(h + 1) * C_VAL_DIM]
            o_ref[rows, h * C_VAL_DIM:(h + 1) * C_VAL_DIM] = (
                o * gain * (g * _sigmoid(g))).astype(o_ref.dtype)


def _retention(q, k, v, g, gain, batch, seq, nchunk=4):
    tb = ATTN_BLOCK * nchunk
    kw, vw = q.shape[1], v.shape[1]
    q3, k3 = q.reshape(batch, seq, kw), k.reshape(batch, seq, kw)
    v3, g3 = v.reshape(batch, seq, vw), g.reshape(batch, seq, vw)
    kspec = pl.BlockSpec((None, tb, kw), lambda b, t: (b, t, 0))
    vspec = pl.BlockSpec((None, tb, vw), lambda b, t: (b, t, 0))
    out = pl.pallas_call(
        functools.partial(_retention_kernel, nchunk=nchunk),
        out_shape=jax.ShapeDtypeStruct((batch, seq, vw), BF16),
        grid=(batch, seq // tb),
        in_specs=[kspec, kspec, vspec, vspec, pl.BlockSpec((1, vw), lambda b, t: (0, 0))],
        out_specs=vspec,
        scratch_shapes=[pltpu.VMEM((C_HEADS, C_KEY_DIM, C_VAL_DIM), F32)],
        compiler_params=_cparams("parallel", "arbitrary"),
        name="retention",
    )(q3, k3, v3, g3, gain.reshape(1, vw).astype(F32))
    return out.reshape(batch * seq, vw)


def _layer_norm(z, g, b):
    mu = jnp.mean(z, axis=1, keepdims=True)
    zc = z - mu
    var = jnp.mean(zc * zc, axis=1, keepdims=True)
    return zc * lax.rsqrt(var + LN_EPS) * g + b


def _outproj_kernel(ya_ref, yb_ref, w_ref, x_ref, g_ref, b_ref, rwh_ref, rwl_ref, rb_ref,
                    x1_ref, idx_ref, gate_ref, *, alpha):
    half = ya_ref.shape[1]
    mix = jnp.dot(ya_ref[...], w_ref[0:half, :], preferred_element_type=F32)
    mix = mix + jnp.dot(yb_ref[...], w_ref[half:, :], preferred_element_type=F32)
    x1 = _layer_norm(alpha * x_ref[...] + mix, g_ref[...], b_ref[...])
    _to_tiles(x1_ref, x1)

    xh = x1.astype(BF16)
    xl = (x1 - xh.astype(F32)).astype(BF16)
    logits = (jnp.dot(xh, rwh_ref[...], preferred_element_type=F32)
              + jnp.dot(xh, rwl_ref[...], preferred_element_type=F32)
              + jnp.dot(xl, rwh_ref[...], preferred_element_type=F32)) + rb_ref[...]
    lane = lax.broadcasted_iota(jnp.int32, logits.shape, 1)
    idx_acc = jnp.zeros(logits.shape, jnp.int32)
    val_acc = jnp.zeros(logits.shape, F32)
    top0 = None
    den = None
    for kk in range(TOP_K):
        m = jnp.max(logits, axis=1, keepdims=True)
        sel = jnp.min(jnp.where(logits == m, lane, LANES), axis=1, keepdims=True)
        if kk == 0:
            top0 = m
        e = jnp.exp(m - top0)
        den = e if den is None else den + e
        idx_acc = jnp.where(lane == kk, sel, idx_acc)
        val_acc = jnp.where(lane == kk, e, val_acc)
        logits = jnp.where(lane == sel, NEG_BIG * 2.0, logits)
    idx_ref[...] = idx_acc[:, 0:idx_ref.shape[1]]
    gate_ref[...] = (val_acc / den)[:, 0:gate_ref.shape[1]]


def _outproj_norm_route(ya, yb, w_out, x, ln_g, ln_b, rw, rb, alpha, tm=512):
    n, d = x.shape
    half = ya.shape[1]
    rw_pad = jnp.zeros((d, LANES), F32).at[:, :N_EXPERTS].set(rw.astype(F32))
    rwh = rw_pad.astype(BF16)
    rwl = (rw_pad - rwh.astype(F32)).astype(BF16)
    rb_pad = jnp.full((1, LANES), NEG_BIG, F32).at[0, :N_EXPERTS].set(rb.astype(F32))
    row = lambda i: (i, 0)
    fixed = lambda i: (0, 0)
    return pl.pallas_call(
        functools.partial(_outproj_kernel, alpha=alpha),
        out_shape=(jax.ShapeDtypeStruct((n * SUBLANES, LANES), F32),
                   jax.ShapeDtypeStruct((n, 8), jnp.int32),
                   jax.ShapeDtypeStruct((n, 8), F32)),
        grid=(n // tm,),
        in_specs=[
            pl.BlockSpec((tm, half), row),
            pl.BlockSpec((tm, half), row),
            pl.BlockSpec((2 * half, d), fixed),
            pl.BlockSpec((tm, d), row),
            pl.BlockSpec((1, d), fixed),
            pl.BlockSpec((1, d), fixed),
            pl.BlockSpec((d, LANES), fixed),
            pl.BlockSpec((d, LANES), fixed),
            pl.BlockSpec((1, LANES), fixed),
        ],
        out_specs=(pl.BlockSpec((tm * SUBLANES, LANES), row),
                   pl.BlockSpec((tm, 8), row),
                   pl.BlockSpec((tm, 8), row)),
        compiler_params=_cparams("parallel"),
        name="outproj_norm_route",
    )(ya, yb, w_out.astype(BF16), x, ln_g.reshape(1, d), ln_b.reshape(1, d), rwh, rwl, rb_pad)


ISSUE_UNROLL = 8


def _issue_tiles(idx_smem, base, src_hbm, dst_ref, sem, count):
    def body(g, carry):
        for u in range(ISSUE_UNROLL):
            r = g * ISSUE_UNROLL + u
            src = pl.multiple_of(idx_smem[base + r] * SUBLANES, SUBLANES)
            dst = pl.multiple_of(r * SUBLANES, SUBLANES)
            pltpu.make_async_copy(src_hbm.at[pl.ds(src, SUBLANES), :],
                                  dst_ref.at[pl.ds(dst, SUBLANES), :], sem).start(priority=u % 2)
        return carry

    lax.fori_loop(0, count // ISSUE_UNROLL, body, 0)


def _drain_tiles(src_hbm, dst_ref, sem, count):
    def body(r, carry):
        dst = pl.multiple_of(r * SUBLANES, SUBLANES)
        pltpu.make_async_copy(src_hbm.at[pl.ds(0, SUBLANES), :],
                              dst_ref.at[pl.ds(dst, SUBLANES), :], sem).wait()
        return carry

    lax.fori_loop(0, count, body, 0, unroll=16)


def _pipelined_gather(idx_hbm, src_hbm, buf, idx_smem, idx_sem, row_sem, count, live):
    i = pl.program_id(0)
    n = pl.num_programs(0)
    slot = i % 2
    nslot = 1 - slot

    def idx_copy(step, s):
        return pltpu.make_async_copy(idx_hbm.at[step], idx_smem.at[pl.ds(s * count, count)],
                                     idx_sem.at[s])

    def issue(step, s):
        @pl.when(live(step))
        def _():
            _issue_tiles(idx_smem, s * count, src_hbm, buf.at[s], row_sem.at[s], count)

    @pl.when(i == 0)
    def _():
        idx_copy(0, 0).start()
        idx_copy(0, 0).wait()
        issue(0, 0)

        @pl.when(n > 1)
        def _():
            idx_copy(1, 1).start()

    @pl.when(i + 1 < n)
    def _():
        idx_copy(i + 1, nslot).wait()
        issue(i + 1, nslot)

    @pl.when(i + 2 < n)
    def _():
        idx_copy(i + 2, slot).start()

    @pl.when(live(i))
    def _():
        _drain_tiles(src_hbm, buf.at[slot], row_sem.at[slot], count)

    return slot


def _moe_kernel(be_ref, nv_ref, rows_hbm, x_hbm, wgu_ref, bgu_ref, wdn_ref, bdn_ref, y_ref,
                xbuf, idx_smem, idx_sem, row_sem, act_ref, wgu_bf, wdn_bf, *, chunk):
    i = pl.program_id(0)
    de = wdn_ref.shape[0]
    slot = _pipelined_gather(rows_hbm, x_hbm, xbuf, idx_smem, idx_sem, row_sem, MOE_BLOCK,
                             lambda step: nv_ref[step] > 0)

    @pl.when((i == 0) | (be_ref[i] != be_ref[jnp.maximum(i - 1, 0)]))
    def _():
        for c0 in range(0, 2 * de, chunk):
            wgu_bf[:, c0:c0 + chunk] = wgu_ref[:, c0:c0 + chunk].astype(BF16)
        for c0 in range(0, wdn_ref.shape[1], chunk):
            wdn_bf[:, c0:c0 + chunk] = wdn_ref[:, c0:c0 + chunk].astype(BF16)

    @pl.when(nv_ref[i] > 0)
    def _():
        xb = _from_tiles(xbuf.at[slot], MOE_BLOCK).astype(BF16)
        for c0 in range(0, de, chunk):
            gate = jnp.dot(xb, wgu_bf[:, c0:c0 + chunk], preferred_element_type=F32)
            gate = gate + bgu_ref[:, c0:c0 + chunk]
            up = jnp.dot(xb, wgu_bf[:, de + c0:de + c0 + chunk], preferred_element_type=F32)
            up = up + bgu_ref[:, de + c0:de + c0 + chunk]
            gate = jnp.minimum(gate, SWIGLU_LIMIT)
            up = jnp.clip(up, -SWIGLU_LIMIT, SWIGLU_LIMIT)
            act = (up + 1.0) * gate * _sigmoid(SWIGLU_ALPHA * gate)
            act_ref[:, c0:c0 + chunk] = act.astype(BF16)
        _to_tiles(y_ref, jnp.dot(act_ref[...], wdn_bf[...], preferred_element_type=F32)
                  + bdn_ref[...])

    @pl.when(nv_ref[i] == 0)
    def _():
        y_ref[...] = jnp.zeros_like(y_ref)


def _moe_experts(x1, rows, block_e, nvalid, w_gu, b_gu, w_dn, b_dn, layer, chunk=512):
    n_exp, de, d = w_dn.shape[1:]
    assert d == SUBLANES * LANES
    nblocks = rows.shape[0]
    cap = nblocks * MOE_BLOCK
    tile_rows = MOE_BLOCK * SUBLANES
    grid_spec = pltpu.PrefetchScalarGridSpec(
        num_scalar_prefetch=2,
        grid=(nblocks,),
        in_specs=[
            pl.BlockSpec(memory_space=pl.ANY),
            pl.BlockSpec(memory_space=pl.ANY),
            pl.BlockSpec((None, None, d, 2 * de), lambda i, be, nv: (layer, be[i], 0, 0)),
            pl.BlockSpec((None, None, 1, 2 * de), lambda i, be, nv: (layer, be[i], 0, 0)),
            pl.BlockSpec((None, None, de, d), lambda i, be, nv: (layer, be[i], 0, 0)),
            pl.BlockSpec((None, None, 1, d), lambda i, be, nv: (layer, be[i], 0, 0)),
        ],
        out_specs=pl.BlockSpec((tile_rows, LANES), lambda i, be, nv: (i, 0)),
        scratch_shapes=[pltpu.VMEM((2, tile_rows, LANES), F32),
                        pltpu.SMEM((2 * MOE_BLOCK,), jnp.int32),
                        pltpu.SemaphoreType.DMA((2,)),
                        pltpu.SemaphoreType.DMA((2,)),
                        pltpu.VMEM((MOE_BLOCK, de), BF16),
                        pltpu.VMEM((d, 2 * de), BF16),
                        pltpu.VMEM((de, d), BF16)],
    )
    depth = w_dn.shape[0]
    return pl.pallas_call(
        functools.partial(_moe_kernel, chunk=chunk),
        out_shape=jax.ShapeDtypeStruct((cap * SUBLANES, LANES), F32),
        grid_spec=grid_spec,
        compiler_params=pltpu.CompilerParams(dimension_semantics=("arbitrary",),
                                             vmem_limit_bytes=MOE_VMEM_LIMIT),
        name="moe_experts",
    )(block_e, nvalid, rows, x1, w_gu, b_gu.reshape(depth, n_exp, 1, 2 * de),
      w_dn, b_dn.reshape(depth, n_exp, 1, d))


def _combine_kernel(pos_hbm, y_hbm, x_ref, gate_ref, g_ref, b_ref, o_ref,
                    buf_ref, pos_smem, pos_sem, row_sem, *, tm, alpha):
    slot = _pipelined_gather(pos_hbm, y_hbm, buf_ref, pos_smem, pos_sem, row_sem, TOP_K * tm,
                             lambda step: step >= 0)
    gates = gate_ref[...]
    z = alpha * _from_tiles(x_ref, tm)
    for kk in range(TOP_K):
        z = z + gates[:, kk:kk + 1] * _from_tiles(buf_ref.at[slot], tm, base=kk * tm * SUBLANES)
    o_ref[...] = _layer_norm(z, g_ref[...], b_ref[...])


def _combine_norm(y, pos, x1, gates, ln_g, ln_b, alpha, tm=256):
    d = SUBLANES * LANES
    n = x1.shape[0] // SUBLANES
    steps = n // tm
    pos_t = pos.reshape(steps, tm, TOP_K).transpose(0, 2, 1).reshape(steps, TOP_K * tm)
    row = lambda i: (i, 0)
    fixed = lambda i: (0, 0)
    return pl.pallas_call(
        functools.partial(_combine_kernel, tm=tm, alpha=alpha),
        out_shape=jax.ShapeDtypeStruct((n, d), F32),
        grid=(steps,),
        in_specs=[
            pl.BlockSpec(memory_space=pl.ANY),
            pl.BlockSpec(memory_space=pl.ANY),
            pl.BlockSpec((tm * SUBLANES, LANES), row),
            pl.BlockSpec((tm, 8), row),
            pl.BlockSpec((1, d), fixed),
            pl.BlockSpec((1, d), fixed),
        ],
        out_specs=pl.BlockSpec((tm, d), row),
        scratch_shapes=[pltpu.VMEM((2, TOP_K * tm * SUBLANES, LANES), F32),
                        pltpu.SMEM((2 * TOP_K * tm,), jnp.int32),
                        pltpu.SemaphoreType.DMA((2,)),
                        pltpu.SemaphoreType.DMA((2,))],
        compiler_params=_cparams("arbitrary"),
        name="combine_norm",
    )(pos_t, y, x1, gates, ln_g.reshape(1, d), ln_b.reshape(1, d))


def _routing_tables(top_idx, n):
    nk = n * TOP_K
    flat_e = top_idx[:, :TOP_K].reshape(nk)
    assign = jnp.arange(nk, dtype=jnp.int32)
    skey = lax.sort(flat_e * nk + assign)
    order = skey % nk
    experts = jnp.arange(N_EXPERTS + 1, dtype=jnp.int32)
    bounds = jnp.sum(((skey // nk)[None, :] < experts[:, None]).astype(jnp.int32), axis=1)
    start, counts = bounds[:-1], bounds[1:] - bounds[:-1]
    _, sorted_at = lax.sort((order, assign), num_keys=1)
    padded = (counts + MOE_BLOCK - 1) // MOE_BLOCK * MOE_BLOCK
    pad_end = jnp.cumsum(padded)
    pad_start = pad_end - padded
    pos = (pad_start[flat_e] + sorted_at - start[flat_e]).astype(jnp.int32)
    n_blocks = nk // MOE_BLOCK + N_EXPERTS + 1
    blk_first = jnp.arange(n_blocks, dtype=jnp.int32) * MOE_BLOCK
    block_e = jnp.sum((pad_end[None, :] <= blk_first[:, None]).astype(jnp.int32), axis=1)
    block_e = jnp.minimum(block_e, N_EXPERTS - 1)
    nvalid = jnp.clip(counts[block_e] - (blk_first - pad_start[block_e]), 0, MOE_BLOCK)
    nvalid = jnp.where(blk_first < pad_end[-1], nvalid, 0).astype(jnp.int32)
    slot = jnp.arange(n_blocks * MOE_BLOCK, dtype=jnp.int32)
    slot_e = jnp.repeat(block_e, MOE_BLOCK)
    within = slot - pad_start[slot_e]
    live = (within < counts[slot_e]) & (slot < pad_end[-1])
    src = order[jnp.clip(start[slot_e] + within, 0, nk - 1)] // TOP_K
    rows = jnp.where(live, src, 0).astype(jnp.int32)
    return rows.reshape(n_blocks, MOE_BLOCK), pos.reshape(n, TOP_K), block_e, nvalid


_EVEN_PLAN = (
    (512, "attn", HEAD_DIM ** -0.5), (128, "attn", 1.0), (128, None, 1.0),
    (512, None, 1.0), (512, None, 1.0), (512, None, 1.0), (512, None, 1.0),
)
_ODD_PLAN = (
    (256, "ret", 1.0), (256, "ret", C_KEY_DIM ** -0.5), (512, None, 1.0), (512, None, 1.0),
) + ((512, "attn", HEAD_DIM ** -0.5), (128, "attn", 1.0), (128, None, 1.0)) * len(D_PATTERNS)


def kernel(x, w_in_even, b_in_even, attn_sinks, hgrn_lb_logits, hgrn_norm, w_out_even, w_in_odd,
           b_in_odd, ret_norm, w_out_odd, ln1_g, ln1_b, ln2_g, ln2_b, router_w, router_b,
           expert_w_gu, expert_b_gu, expert_w_dn, expert_b_dn):
    batch, seq, d = x.shape
    n = batch * seq
    depth = ln1_g.shape[0]
    alpha = float((2 * depth) ** 0.25)

    rope_inv = 1.0 / (ROPE_THETA ** (jnp.arange(0, ROPE_DIM, 2, dtype=F32) / ROPE_DIM))
    ret_inv = 1.0 / (RET_THETA ** jnp.linspace(0.0, 1.0, C_KEY_DIM // 2, dtype=F32))
    tab_attn = _rope_tables(seq, ROPE_DIM // 2, rope_inv, HEAD_DIM)
    tab_ret = _rope_tables(seq, C_KEY_DIM // 2, ret_inv, C_KEY_DIM)
    lb_soft = jax.nn.softmax(hgrn_lb_logits.astype(F32), axis=0)
    lower_bounds = jnp.concatenate(
        [jnp.zeros_like(lb_soft[:1]), jnp.cumsum(lb_soft, axis=0)[:-1]], axis=0)

    xf = x.reshape(n, d)
    for layer in range(depth):
        j = layer // 2
        if layer % 2 == 0:
            aq, ak, av, bq, bf, bi, bg = _inproj(
                xf, w_in_even[j].astype(BF16), b_in_even[j].reshape(1, -1), tab_attn, tab_ret,
                _EVEN_PLAN, seq)
            (ya,) = _banded_attention(aq, ak, av, batch, seq, 1, A_WINDOW - 1,
                                      sinks=attn_sinks[j])
            yb = _hgrn2(bq, bf, bi, bg, lower_bounds[j], hgrn_norm[j], batch, seq)
            w_out = w_out_even[j]
        else:
            outs = _inproj(xf, w_in_odd[j].astype(BF16), b_in_odd[j].reshape(1, -1), tab_attn,
                           tab_ret, _ODD_PLAN, seq)
            cq, ck, cv, cg = outs[:4]
            ya = _retention(cq, ck, cv, cg, ret_norm[j], batch, seq)
            parts = []
            for p, (window, dil) in enumerate(D_PATTERNS):
                dq, dk, dv = outs[4 + 3 * p:7 + 3 * p]
                o, lse = _banded_attention(dq, dk, dv, batch, seq, dil, window // dil,
                                           want_lse=True, out_dtype=F32)
                parts += [o, lse]
            yb = _merge_patterns(parts)
            w_out = w_out_odd[j]
        x1, top_idx, gates = _outproj_norm_route(
            ya, yb, w_out, xf, ln1_g[layer], ln1_b[layer], router_w[layer], router_b[layer], alpha)
        rows, pos, block_e, nvalid = _routing_tables(top_idx, n)
        y = _moe_experts(x1, rows, block_e, nvalid, expert_w_gu, expert_b_gu, expert_w_dn,
                         expert_b_dn, layer)
        xf = _combine_norm(y, pos, x1, gates, ln2_g[layer], ln2_b[layer], alpha)
    return xf.reshape(batch, seq, d)
```

```python
import functools
import math

import jax
import jax.numpy as jnp
from jax import lax
from jax.experimental import pallas as pl
from jax.experimental.pallas import tpu as pltpu

F32 = jnp.float32
BF16 = jnp.bfloat16

HEAD_DIM = 64
ATTN_BLOCK = 128
A_Q_HEADS = 8
A_KV_HEADS = 2
A_WINDOW = 128
B_HEADS = 4
B_KEY_DIM = 128
B_MIN_F = 1e-30
C_HEADS = 4
C_KEY_DIM = 64
C_VAL_DIM = 128
D_PATTERNS = ((128, 1), (512, 4), (2048, 16))
ROPE_THETA = 500000.0
ROPE_DIM = HEAD_DIM // 4
RET_THETA = 10000.0
N_EXPERTS = 32
TOP_K = 4
SWIGLU_LIMIT = 7.0
SWIGLU_ALPHA = 1.702
MOE_BLOCK = 512
LN_EPS = 1e-5
NORM_EPS = 1e-6
NEG_BIG = -1e30

LANES = 128
SUBLANES = 8
VMEM_LIMIT = 48 * 1024 * 1024
MOE_VMEM_LIMIT = 56 * 1024 * 1024

HGRN_CHUNK = 128
HGRN_SUB = 8
HGRN_LEVELS = (16, 32, 64, 128)


def _cparams(*sem):
    return pltpu.CompilerParams(dimension_semantics=sem, vmem_limit_bytes=VMEM_LIMIT)


def _to_tiles(ref, value):
    rows = value.shape[0]
    for c in range(SUBLANES):
        ref[pl.ds(c, rows, stride=SUBLANES), :] = value[:, c * LANES:(c + 1) * LANES]


def _from_tiles(ref, rows, base=0):
    return jnp.concatenate(
        [ref[pl.ds(base + c, rows, stride=SUBLANES), :] for c in range(SUBLANES)], axis=1)


def _sigmoid(z):
    return 1.0 / (1.0 + jnp.exp(-z))


def _rotate(h, cos, sa, sb, shift):
    width = h.shape[1]
    rep = width // LANES
    if rep > 1:
        cos = jnp.concatenate([cos] * rep, axis=1)
        sa = jnp.concatenate([sa] * rep, axis=1)
        sb = jnp.concatenate([sb] * rep, axis=1)
    up = pltpu.roll(h, width - shift, axis=1)
    dn = pltpu.roll(h, shift, axis=1)
    return h * cos + up * sa + dn * sb


def _inproj_kernel(x_ref, w_ref, b_ref, ta_ref, tr_ref, *out_refs, plan):
    xb = x_ref[...].astype(BF16)
    off = 0
    for o_ref, (width, rope, scale) in zip(out_refs, plan):
        h = jnp.dot(xb, w_ref[:, off:off + width], preferred_element_type=F32)
        h = h + b_ref[:, off:off + width]
        if rope == "attn":
            h = _rotate(h, ta_ref[0], ta_ref[1], ta_ref[2], ROPE_DIM // 2)
        elif rope == "ret":
            h = _rotate(h, tr_ref[0], tr_ref[1], tr_ref[2], C_KEY_DIM // 2)
        if scale != 1.0:
            h = h * scale
        o_ref[...] = h.astype(o_ref.dtype)
        off += width


def _inproj(x, w, b, tab_attn, tab_ret, plan, seq, tm=512):
    n, d = x.shape
    cols = w.shape[1]
    tpb = seq // tm
    kern = functools.partial(_inproj_kernel, plan=plan)
    outs = tuple(jax.ShapeDtypeStruct((n, wd), F32) for wd, _, _ in plan)
    return pl.pallas_call(
        kern,
        out_shape=outs,
        grid=(n // tm,),
        in_specs=[
            pl.BlockSpec((tm, d), lambda i: (i, 0)),
            pl.BlockSpec((d, cols), lambda i: (0, 0)),
            pl.BlockSpec((1, cols), lambda i: (0, 0)),
            pl.BlockSpec((3, tm, LANES), lambda i: (0, i % tpb, 0)),
            pl.BlockSpec((3, tm, LANES), lambda i: (0, i % tpb, 0)),
        ],
        out_specs=tuple(pl.BlockSpec((tm, wd), lambda i: (i, 0)) for wd, _, _ in plan),
        compiler_params=_cparams("parallel"),
        name="inproj",
    )(x, w, b, tab_attn, tab_ret)


def _rope_tables(seq, half, inv_freq, period):
    pos = jnp.arange(seq, dtype=F32)
    ang = pos[:, None] * inv_freq[None, :]
    cos, sin = jnp.cos(ang), jnp.sin(ang)
    pad = period - 2 * half
    ones = jnp.ones((seq, pad), F32)
    zeros = jnp.zeros((seq, pad), F32)
    zh = jnp.zeros((seq, half), F32)
    cosf = jnp.concatenate([cos, cos, ones], axis=1)
    sa = jnp.concatenate([-sin, zh, zeros], axis=1)
    sb = jnp.concatenate([zh, sin, zeros], axis=1)
    rep = LANES // period
    return jnp.stack([jnp.tile(t, (1, rep)) for t in (cosf, sa, sb)], axis=0)


def _attn_kernel(*refs, dil, max_dist, nsub, has_sink, want_lse):
    if has_sink:
        sink_ref, refs = refs[0], refs[1:]
    q_refs = refs[:4]
    k_ref, v_ref, kp_ref, vp_ref = refs[4:8]
    o_ref = refs[8]
    lse_ref = refs[9] if want_lse else None
    n_out = 2 if want_lse else 1
    stage = refs[8 + n_out:]
    blk = ATTN_BLOCK
    qi = lax.broadcasted_iota(jnp.int32, (4 * blk, 2 * blk), 0) % blk
    kj = lax.broadcasted_iota(jnp.int32, (4 * blk, 2 * blk), 1)
    dist = qi - kj + blk
    band = (dist >= 0) & (dist <= max_dist)
    first_kmin = jnp.where(pl.program_id(1) == 0, blk, 0)
    band_first = band & (kj >= first_kmin)
    row4 = lax.broadcasted_iota(jnp.int32, (4 * blk, 1), 0) // blk

    def take(start):
        return pl.ds(start, blk, stride=dil) if dil > 1 else pl.ds(start, blk)

    def one_group(r, j):
        rows = take(j * blk * dil + r)
        if j == 0:
            k_prev, v_prev = kp_ref[take(r), :], vp_ref[take(r), :]
            valid = band_first
        else:
            prow = take((j - 1) * blk * dil + r)
            k_prev, v_prev = k_ref[prow, :], v_ref[prow, :]
            valid = band
        k_cat = jnp.concatenate([k_prev, k_ref[rows, :]], axis=0).astype(BF16)
        v_cat = jnp.concatenate([v_prev, v_ref[rows, :]], axis=0).astype(BF16)
        for h in range(A_KV_HEADS):
            qa, qb = q_refs[2 * h][rows, :], q_refs[2 * h + 1][rows, :]
            q4 = jnp.concatenate([qa[:, :HEAD_DIM], qa[:, HEAD_DIM:], qb[:, :HEAD_DIM],
                                  qb[:, HEAD_DIM:]], axis=0).astype(BF16)
            kh = k_cat[:, h * HEAD_DIM:(h + 1) * HEAD_DIM]
            vh = v_cat[:, h * HEAD_DIM:(h + 1) * HEAD_DIM]
            s = lax.dot_general(q4, kh, (((1,), (1,)), ((), ())), preferred_element_type=F32)
            s = jnp.where(valid, s, NEG_BIG)
            m = jnp.max(s, axis=1, keepdims=True)
            if has_sink:
                sk = jnp.full((4 * blk, 1), sink_ref[4 * h + 3], F32)
                for g in range(3):
                    sk = jnp.where(row4 == g, sink_ref[4 * h + g], sk)
                m = jnp.maximum(m, sk)
            p = jnp.exp(s - m)
            den = jnp.sum(p, axis=1, keepdims=True)
            if has_sink:
                den = den + jnp.exp(sk - m)
            o = jnp.dot(p.astype(BF16), vh, preferred_element_type=F32) / den
            lse = m + jnp.log(den) if want_lse else None
            for half in range(2):
                cb = 2 * h + half
                g0 = 2 * half
                pair = jnp.concatenate(
                    [o[g0 * blk:(g0 + 1) * blk], o[(g0 + 1) * blk:(g0 + 2) * blk]], axis=1)
                if dil > 1:
                    stage[0][cb, rows, :] = pair
                else:
                    o_ref[rows, cb * LANES:(cb + 1) * LANES] = pair.astype(o_ref.dtype)
                if want_lse:
                    lpair = jnp.concatenate(
                        [jnp.broadcast_to(lse[g * blk:(g + 1) * blk], (blk, HEAD_DIM))
                         for g in (g0, g0 + 1)], axis=1)
                    if dil > 1:
                        stage[1][cb, rows, :] = lpair
                    else:
                        lse_ref[rows, cb * LANES:(cb + 1) * LANES] = lpair

    for j in range(nsub):
        if dil == 1:
            one_group(0, j)
        else:
            def body(r, carry, j=j):
                one_group(r, j)
                return carry
            lax.fori_loop(0, dil, body, 0)
    if dil > 1:
        for cb in range(4):
            o_ref[:, cb * LANES:(cb + 1) * LANES] = stage[0][cb].astype(o_ref.dtype)
            if want_lse:
                lse_ref[:, cb * LANES:(cb + 1) * LANES] = stage[1][cb]


def _banded_attention(q, k, v, batch, seq, dil, max_dist, sinks=None, want_lse=False,
                      out_dtype=BF16):
    span = ATTN_BLOCK * dil
    nsub = max(1, 512 // span)
    tb = span * nsub
    nt = seq // tb
    qw, kw = q.shape[1], k.shape[1]
    q3, k3, v3 = (t.reshape(batch, seq, t.shape[1]) for t in (q, k, v))
    kern = functools.partial(_attn_kernel, dil=dil, max_dist=max_dist, nsub=nsub,
                             has_sink=sinks is not None, want_lse=want_lse)
    cur = lambda b, t: (b, t, 0)
    prev = lambda b, t: (b, jnp.maximum(t * nsub - 1, 0), 0)
    in_specs = [pl.BlockSpec((None, tb, LANES), lambda b, t, c=c: (b, t, c)) for c in range(4)] + [
        pl.BlockSpec((None, tb, kw), cur),
        pl.BlockSpec((None, tb, kw), cur),
        pl.BlockSpec((None, span, kw), prev),
        pl.BlockSpec((None, span, kw), prev),
    ]
    args = [q3, q3, q3, q3, k3, v3, k3, v3]
    if sinks is not None:
        in_specs = [pl.BlockSpec(memory_space=pltpu.SMEM)] + in_specs
        args = [sinks.astype(F32)] + args
    n_out = 2 if want_lse else 1
    out_shape = [jax.ShapeDtypeStruct((batch, seq, qw), out_dtype)]
    out_specs = [pl.BlockSpec((None, tb, qw), cur)]
    if want_lse:
        out_shape.append(jax.ShapeDtypeStruct((batch, seq, qw), F32))
        out_specs.append(pl.BlockSpec((None, tb, qw), cur))
    scratch = [pltpu.VMEM((4, tb, LANES), F32)] * n_out if dil > 1 else []
    res = pl.pallas_call(
        kern,
        out_shape=tuple(out_shape),
        grid=(batch, nt),
        in_specs=in_specs,
        out_specs=tuple(out_specs),
        scratch_shapes=scratch,
        compiler_params=_cparams("parallel", "arbitrary"),
        name=f"banded_attn_d{dil}",
    )(*args)
    return tuple(r.reshape(batch * seq, qw) for r in res)


def _merge_kernel(o1, l1, o2, l2, o3, l3, out_ref):
    a, b, c = l1[...], l2[...], l3[...]
    m = jnp.maximum(jnp.maximum(a, b), c)
    ea, eb, ec = jnp.exp(a - m), jnp.exp(b - m), jnp.exp(c - m)
    num = ea * o1[...] + eb * o2[...] + ec * o3[...]
    out_ref[...] = (num / (ea + eb + ec)).astype(out_ref.dtype)


def _merge_patterns(parts, tm=512):
    n, w = parts[0].shape
    spec = pl.BlockSpec((tm, w), lambda i: (i, 0))
    return pl.pallas_call(
        _merge_kernel,
        out_shape=jax.ShapeDtypeStruct((n, w), BF16),
        grid=(n // tm,),
        in_specs=[spec] * 6,
        out_specs=spec,
        compiler_params=_cparams("parallel"),
        name="merge_patterns",
    )(*parts)


def _hgrn_structure():
    c = HGRN_CHUNK
    r = lax.broadcasted_iota(jnp.int32, (c, c), 0)
    j = lax.broadcasted_iota(jnp.int32, (c, c), 1)
    blocks = [(j <= r)]
    for lv in HGRN_LEVELS:
        mid = (r // lv) * lv + lv // 2 - 1
        second = (r % lv) >= lv // 2
        lo = jnp.where(second, mid, r)
        hi = jnp.where(second, r, mid)
        blocks.append((j > lo) & (j <= hi))
    return jnp.concatenate([jnp.where(m, 1.0, 0.0) for m in blocks], axis=0).astype(BF16)


def _split3(x):
    hi = x.astype(BF16)
    r1 = x - hi.astype(F32)
    mid = r1.astype(BF16)
    lo = (r1 - mid.astype(F32)).astype(BF16)
    return hi, mid, lo


def _hgrn_kernel(q_ref, f_ref, i_ref, g_ref, lb_ref, gain_ref, o_ref, st_ref, *, nchunk):
    c = HGRN_CHUNK

    @pl.when(pl.program_id(2) == 0)
    def _():
        st_ref[...] = jnp.zeros_like(st_ref)

    dmat = _hgrn_structure()
    row = lax.broadcasted_iota(jnp.int32, (c, c), 0)
    col = lax.broadcasted_iota(jnp.int32, (c, c), 1)
    lb = jnp.clip(lb_ref[...], 0.0, 1.0)
    gain = gain_ref[...]
    sub_row = lax.broadcasted_iota(jnp.int32, (HGRN_SUB, c), 0)
    sub_col = lax.broadcasted_iota(jnp.int32, (HGRN_SUB, c), 1)

    for ci in range(nchunk):
        rows = pl.ds(ci * c, c)
        q = q_ref[rows, :]
        f = lb + (1.0 - lb) * _sigmoid(f_ref[rows, :])
        lf = jnp.log(jnp.maximum(f, B_MIN_F))
        key = 1.0 - f
        val = i_ref[rows, :].astype(BF16)

        hi, mid, lo = _split3(lf)
        sums = (jnp.dot(dmat, hi, preferred_element_type=F32)
                + jnp.dot(dmat, mid, preferred_element_type=F32)
                + jnp.dot(dmat, lo, preferred_element_type=F32))
        b = sums[0:c]

        att = jnp.zeros((c, c), F32)
        for n, lv in enumerate(HGRN_LEVELS):
            e = jnp.exp(sums[(n + 1) * c:(n + 2) * c])
            second = (row % lv) >= lv // 2
            qs = jnp.where(second, q * e, 0.0).astype(BF16)
            ks = jnp.where(second, 0.0, key * e).astype(BF16)
            a = lax.dot_general(qs, ks, (((1,), (1,)), ((), ())), preferred_element_type=F32)
            att = att + jnp.where((row // lv) == (col // lv), a, 0.0)

        diag = []
        for a0 in range(0, c, HGRN_SUB):
            qa, ka, ba = q[a0:a0 + HGRN_SUB], key[a0:a0 + HGRN_SUB], b[a0:a0 + HGRN_SUB]
            acc = jnp.zeros((HGRN_SUB, c), F32)
            for s in range(HGRN_SUB):
                e = jnp.exp(jnp.minimum(ba - ba[s:s + 1, :], 0.0))
                w = jnp.sum(qa * e * ka[s:s + 1, :], axis=1, keepdims=True)
                acc = jnp.where((sub_col == a0 + s) & (sub_row >= s), w, acc)
            diag.append(acc)
        att = att + jnp.concatenate(diag, axis=0)

        st = st_ref[...]
        b_last = b[c - 1:c, :]
        q_in = (q * jnp.exp(b)).astype(BF16)
        k_out = (key * jnp.exp(b_last - b)).astype(BF16)
        o = jnp.dot(att.astype(BF16), val, preferred_element_type=F32)
        o = o + lax.dot_general(q_in, st.astype(BF16), (((1,), (1,)), ((), ())),
                                preferred_element_type=F32)
        st_ref[...] = jnp.exp(b_last) * st + lax.dot_general(
            val, k_out, (((0,), (0,)), ((), ())), preferred_element_type=F32)

        o = o * lax.rsqrt(jnp.mean(o * o, axis=1, keepdims=True) + NORM_EPS)
        g = g_ref[rows, :]
        o_ref[rows, :] = (o * gain * (g * _sigmoid(g))).astype(o_ref.dtype)


def _hgrn2(q, f, i_in, g, lb, gain, batch, seq, nchunk=4):
    width = q.shape[1]
    heads = width // B_KEY_DIM
    tb = HGRN_CHUNK * nchunk
    q3, f3, i3, g3 = (t.reshape(batch, seq, width) for t in (q, f, i_in, g))
    blk = pl.BlockSpec((None, tb, B_KEY_DIM), lambda b, h, t: (b, t, h))
    vec = pl.BlockSpec((1, B_KEY_DIM), lambda b, h, t: (0, h))
    out = pl.pallas_call(
        functools.partial(_hgrn_kernel, nchunk=nchunk),
        out_shape=jax.ShapeDtypeStruct((batch, seq, width), BF16),
        grid=(batch, heads, seq // tb),
        in_specs=[blk, blk, blk, blk, vec, vec],
        out_specs=blk,
        scratch_shapes=[pltpu.VMEM((B_KEY_DIM, B_KEY_DIM), F32)],
        compiler_params=_cparams("parallel", "parallel", "arbitrary"),
        name="hgrn2",
    )(q3, f3, i3, g3, lb.reshape(1, width).astype(F32), gain.reshape(1, width).astype(F32))
    return out.reshape(batch * seq, width)


def _retention_kernel(q_ref, k_ref, v_ref, g_ref, gain_ref, o_ref, st_ref, *, nchunk):
    c = ATTN_BLOCK

    @pl.when(pl.program_id(1) == 0)
    def _():
        st_ref[...] = jnp.zeros_like(st_ref)

    ri = lax.broadcasted_iota(jnp.int32, (c, c), 0)
    ci_ = lax.broadcasted_iota(jnp.int32, (c, c), 1)
    rel = (ri - ci_).astype(F32)
    rowf = ri.astype(F32)

    for h in range(C_HEADS):
        log_gamma = math.log1p(-(2.0 ** (-5.0 - h)))
        decay = jnp.where(rel >= 0.0, jnp.exp(log_gamma * jnp.maximum(rel, 0.0)), 0.0)
        head_scale = jnp.exp(log_gamma * (rowf + 1.0))
        tail_scale = jnp.exp(log_gamma * (c - 1.0 - rowf))
        chunk_decay = math.exp(log_gamma * c)
        gain = gain_ref[:, h * C_VAL_DIM:(h + 1) * C_VAL_DIM]
        for ci in range(nchunk):
            rows = pl.ds(ci * c, c)
            q = q_ref[rows, h * C_KEY_DIM:(h + 1) * C_KEY_DIM].astype(BF16)
            k = k_ref[rows, h * C_KEY_DIM:(h + 1) * C_KEY_DIM].astype(BF16)
            v = v_ref[rows, h * C_VAL_DIM:(h + 1) * C_VAL_DIM]
            st = st_ref[h]
            s = lax.dot_general(q, k, (((1,), (1,)), ((), ())), preferred_element_type=F32)
            o = jnp.dot((s * decay).astype(BF16), v.astype(BF16), preferred_element_type=F32)
            o = o + head_scale * jnp.dot(q, st.astype(BF16), preferred_element_type=F32)
            st_ref[h] = chunk_decay * st + lax.dot_general(
                k, (v * tail_scale).astype(BF16), (((0,), (0,)), ((), ())),
                preferred_element_type=F32)
            o = o - jnp.mean(o, axis=1, keepdims=True)
            o = o * lax.rsqrt(jnp.mean(o * o, axis=1, keepdims=True) + NORM_EPS)
            g = g_ref[rows, h * C_VAL_DIM:(h + 1) * C_VAL_DIM]
            o_ref[rows, h * C_VAL_DIM:(h + 1) * C_VAL_DIM] = (
                o * gain * (g * _sigmoid(g))).astype(o_ref.dtype)


def _retention(q, k, v, g, gain, batch, seq, nchunk=4):
    tb = ATTN_BLOCK * nchunk
    kw, vw = q.shape[1], v.shape[1]
    q3, k3 = q.reshape(batch, seq, kw), k.reshape(batch, seq, kw)
    v3, g3 = v.reshape(batch, seq, vw), g.reshape(batch, seq, vw)
    kspec = pl.BlockSpec((None, tb, kw), lambda b, t: (b, t, 0))
    vspec = pl.BlockSpec((None, tb, vw), lambda b, t: (b, t, 0))
    out = pl.pallas_call(
        functools.partial(_retention_kernel, nchunk=nchunk),
        out_shape=jax.ShapeDtypeStruct((batch, seq, vw), BF16),
        grid=(batch, seq // tb),
        in_specs=[kspec, kspec, vspec, vspec, pl.BlockSpec((1, vw), lambda b, t: (0, 0))],
        out_specs=vspec,
        scratch_shapes=[pltpu.VMEM((C_HEADS, C_KEY_DIM, C_VAL_DIM), F32)],
        compiler_params=_cparams("parallel", "arbitrary"),
        name="retention",
    )(q3, k3, v3, g3, gain.reshape(1, vw).astype(F32))
    return out.reshape(batch * seq, vw)


def _layer_norm(z, g, b):
    mu = jnp.mean(z, axis=1, keepdims=True)
    zc = z - mu
    var = jnp.mean(zc * zc, axis=1, keepdims=True)
    return zc * lax.rsqrt(var + LN_EPS) * g + b


def _outproj_kernel(ya_ref, yb_ref, w_ref, x_ref, g_ref, b_ref, rwh_ref, rwl_ref, rb_ref,
                    x1_ref, idx_ref, gate_ref, *, alpha):
    half = ya_ref.shape[1]
    mix = jnp.dot(ya_ref[...], w_ref[0:half, :], preferred_element_type=F32)
    mix = mix + jnp.dot(yb_ref[...], w_ref[half:, :], preferred_element_type=F32)
    x1 = _layer_norm(alpha * x_ref[...] + mix, g_ref[...], b_ref[...])
    _to_tiles(x1_ref, x1)

    xh = x1.astype(BF16)
    xl = (x1 - xh.astype(F32)).astype(BF16)
    logits = (jnp.dot(xh, rwh_ref[...], preferred_element_type=F32)
              + jnp.dot(xh, rwl_ref[...], preferred_element_type=F32)
              + jnp.dot(xl, rwh_ref[...], preferred_element_type=F32)) + rb_ref[...]
    lane = lax.broadcasted_iota(jnp.int32, logits.shape, 1)
    idx_acc = jnp.zeros(logits.shape, jnp.int32)
    val_acc = jnp.zeros(logits.shape, F32)
    top0 = None
    den = None
    for kk in range(TOP_K):
        m = jnp.max(logits, axis=1, keepdims=True)
        sel = jnp.min(jnp.where(logits == m, lane, LANES), axis=1, keepdims=True)
        if kk == 0:
            top0 = m
        e = jnp.exp(m - top0)
        den = e if den is None else den + e
        idx_acc = jnp.where(lane == kk, sel, idx_acc)
        val_acc = jnp.where(lane == kk, e, val_acc)
        logits = jnp.where(lane == sel, NEG_BIG * 2.0, logits)
    idx_ref[...] = idx_acc[:, 0:idx_ref.shape[1]]
    gate_ref[...] = (val_acc / den)[:, 0:gate_ref.shape[1]]


def _outproj_norm_route(ya, yb, w_out, x, ln_g, ln_b, rw, rb, alpha, tm=512):
    n, d = x.shape
    half = ya.shape[1]
    rw_pad = jnp.zeros((d, LANES), F32).at[:, :N_EXPERTS].set(rw.astype(F32))
    rwh = rw_pad.astype(BF16)
    rwl = (rw_pad - rwh.astype(F32)).astype(BF16)
    rb_pad = jnp.full((1, LANES), NEG_BIG, F32).at[0, :N_EXPERTS].set(rb.astype(F32))
    row = lambda i: (i, 0)
    fixed = lambda i: (0, 0)
    return pl.pallas_call(
        functools.partial(_outproj_kernel, alpha=alpha),
        out_shape=(jax.ShapeDtypeStruct((n * SUBLANES, LANES), F32),
                   jax.ShapeDtypeStruct((n, 8), jnp.int32),
                   jax.ShapeDtypeStruct((n, 8), F32)),
        grid=(n // tm,),
        in_specs=[
            pl.BlockSpec((tm, half), row),
            pl.BlockSpec((tm, half), row),
            pl.BlockSpec((2 * half, d), fixed),
            pl.BlockSpec((tm, d), row),
            pl.BlockSpec((1, d), fixed),
            pl.BlockSpec((1, d), fixed),
            pl.BlockSpec((d, LANES), fixed),
            pl.BlockSpec((d, LANES), fixed),
            pl.BlockSpec((1, LANES), fixed),
        ],
        out_specs=(pl.BlockSpec((tm * SUBLANES, LANES), row),
                   pl.BlockSpec((tm, 8), row),
                   pl.BlockSpec((tm, 8), row)),
        compiler_params=_cparams("parallel"),
        name="outproj_norm_route",
    )(ya, yb, w_out.astype(BF16), x, ln_g.reshape(1, d), ln_b.reshape(1, d), rwh, rwl, rb_pad)


ISSUE_UNROLL = 8


def _issue_tiles(idx_smem, base, src_hbm, dst_ref, sem, count):
    def body(g, carry):
        for u in range(ISSUE_UNROLL):
            r = g * ISSUE_UNROLL + u
            src = pl.multiple_of(idx_smem[base + r] * SUBLANES, SUBLANES)
            dst = pl.multiple_of(r * SUBLANES, SUBLANES)
            pltpu.make_async_copy(src_hbm.at[pl.ds(src, SUBLANES), :],
                                  dst_ref.at[pl.ds(dst, SUBLANES), :], sem).start(priority=u % 2)
        return carry

    lax.fori_loop(0, count // ISSUE_UNROLL, body, 0)


def _drain_tiles(src_hbm, dst_ref, sem, count):
    def body(r, carry):
        dst = pl.multiple_of(r * SUBLANES, SUBLANES)
        pltpu.make_async_copy(src_hbm.at[pl.ds(0, SUBLANES), :],
                              dst_ref.at[pl.ds(dst, SUBLANES), :], sem).wait()
        return carry

    lax.fori_loop(0, count, body, 0, unroll=16)


def _pipelined_gather(idx_hbm, src_hbm, buf, idx_smem, idx_sem, row_sem, count, live):
    i = pl.program_id(0)
    n = pl.num_programs(0)
    slot = i % 2
    nslot = 1 - slot

    def idx_copy(step, s):
        return pltpu.make_async_copy(idx_hbm.at[step], idx_smem.at[pl.ds(s * count, count)],
                                     idx_sem.at[s])

    def issue(step, s):
        @pl.when(live(step))
        def _():
            _issue_tiles(idx_smem, s * count, src_hbm, buf.at[s], row_sem.at[s], count)

    @pl.when(i == 0)
    def _():
        idx_copy(0, 0).start()
        idx_copy(0, 0).wait()
        issue(0, 0)

        @pl.when(n > 1)
        def _():
            idx_copy(1, 1).start()

    @pl.when(i + 1 < n)
    def _():
        idx_copy(i + 1, nslot).wait()
        issue(i + 1, nslot)

    @pl.when(i + 2 < n)
    def _():
        idx_copy(i + 2, slot).start()

    @pl.when(live(i))
    def _():
        _drain_tiles(src_hbm, buf.at[slot], row_sem.at[slot], count)

    return slot


def _moe_kernel(be_ref, nv_ref, rows_hbm, x_hbm, wgu_ref, bgu_ref, wdn_ref, bdn_ref, y_ref,
                xbuf, idx_smem, idx_sem, row_sem, act_ref, wgu_bf, wdn_bf, *, chunk):
    i = pl.program_id(0)
    n = pl.num_programs(0)
    de, d = wdn_ref.shape
    count = MOE_BLOCK
    slot = i % 2
    nslot = 1 - slot

    def idx_copy(step, s):
        return pltpu.make_async_copy(rows_hbm.at[step], idx_smem.at[pl.ds(s * count, count)],
                                     idx_sem.at[s])

    @pl.when(i == 0)
    def _():
        idx_copy(0, 0).start()
        idx_copy(0, 0).wait()
        _issue_tiles(idx_smem, 0, x_hbm, xbuf.at[0], row_sem.at[0], count)

        @pl.when(n > 1)
        def _():
            idx_copy(1, 1).start()

    @pl.when(i + 1 < n)
    def _():
        idx_copy(i + 1, nslot).wait()

    @pl.when(i + 2 < n)
    def _():
        idx_copy(i + 2, slot).start()

    @pl.when((i == 0) | (nv_ref[jnp.maximum(i - 1, 0)] > 0))
    def _():
        _drain_tiles(x_hbm, xbuf.at[slot], row_sem.at[slot], count)

    def issue_next(piece, pieces):
        per = count // pieces
        for r in range(piece * per, (piece + 1) * per):
            src = pl.multiple_of(idx_smem[nslot * count + r] * SUBLANES, SUBLANES)
            pltpu.make_async_copy(
                x_hbm.at[pl.ds(src, SUBLANES), :],
                xbuf.at[nslot, pl.ds(r * SUBLANES, SUBLANES), :],
                row_sem.at[nslot]).start(priority=r % 2)

    @pl.when((i == 0) | (be_ref[i] != be_ref[jnp.maximum(i - 1, 0)]))
    def _():
        for c0 in range(0, 2 * de, chunk):
            wgu_bf[:, c0:c0 + chunk] = wgu_ref[:, c0:c0 + chunk].astype(BF16)
        for c0 in range(0, wdn_ref.shape[1], chunk):
            wdn_bf[:, c0:c0 + chunk] = wdn_ref[:, c0:c0 + chunk].astype(BF16)

    @pl.when(nv_ref[i] > 0)
    def _():
        xb = _from_tiles(xbuf.at[slot], MOE_BLOCK).astype(BF16)
        pieces = de // chunk + d // chunk
        for p, c0 in enumerate(range(0, de, chunk)):
            issue_next(p, pieces)
            gate = jnp.dot(xb, wgu_bf[:, c0:c0 + chunk], preferred_element_type=F32)
            gate = gate + bgu_ref[:, c0:c0 + chunk]
            up = jnp.dot(xb, wgu_bf[:, de + c0:de + c0 + chunk], preferred_element_type=F32)
            up = up + bgu_ref[:, de + c0:de + c0 + chunk]
            gate = jnp.minimum(gate, SWIGLU_LIMIT)
            up = jnp.clip(up, -SWIGLU_LIMIT, SWIGLU_LIMIT)
            act = (up + 1.0) * gate * _sigmoid(SWIGLU_ALPHA * gate)
            act_ref[:, c0:c0 + chunk] = act.astype(BF16)
        for p, n0 in enumerate(range(0, d, chunk)):
            issue_next(de // chunk + p, pieces)
            yp = jnp.dot(act_ref[...], wdn_bf[:, n0:n0 + chunk], preferred_element_type=F32)
            yp = yp + bdn_ref[:, n0:n0 + chunk]
            for c in range(chunk // LANES):
                y_ref[pl.ds(n0 // LANES + c, count, stride=SUBLANES), :] = (
                    yp[:, c * LANES:(c + 1) * LANES])

    @pl.when(nv_ref[i] == 0)
    def _():
        y_ref[...] = jnp.zeros_like(y_ref)


def _moe_experts(x1, rows, block_e, nvalid, w_gu, b_gu, w_dn, b_dn, layer, chunk=256):
    n_exp, de, d = w_dn.shape[1:]
    assert d == SUBLANES * LANES
    nblocks = rows.shape[0]
    cap = nblocks * MOE_BLOCK
    assert nblocks >= (x1.shape[0] // SUBLANES) * TOP_K // MOE_BLOCK + n_exp + 1
    tile_rows = MOE_BLOCK * SUBLANES
    grid_spec = pltpu.PrefetchScalarGridSpec(
        num_scalar_prefetch=2,
        grid=(nblocks,),
        in_specs=[
            pl.BlockSpec(memory_space=pl.ANY),
            pl.BlockSpec(memory_space=pl.ANY),
            pl.BlockSpec((None, None, d, 2 * de), lambda i, be, nv: (layer, be[i], 0, 0)),
            pl.BlockSpec((None, None, 1, 2 * de), lambda i, be, nv: (layer, be[i], 0, 0)),
            pl.BlockSpec((None, None, de, d), lambda i, be, nv: (layer, be[i], 0, 0)),
            pl.BlockSpec((None, None, 1, d), lambda i, be, nv: (layer, be[i], 0, 0)),
        ],
        out_specs=pl.BlockSpec((tile_rows, LANES), lambda i, be, nv: (i, 0)),
        scratch_shapes=[pltpu.VMEM((2, tile_rows, LANES), F32),
                        pltpu.SMEM((2 * MOE_BLOCK,), jnp.int32),
                        pltpu.SemaphoreType.DMA((2,)),
                        pltpu.SemaphoreType.DMA((2,)),
                        pltpu.VMEM((MOE_BLOCK, de), BF16),
                        pltpu.VMEM((d, 2 * de), BF16),
                        pltpu.VMEM((de, d), BF16)],
    )
    depth = w_dn.shape[0]
    return pl.pallas_call(
        functools.partial(_moe_kernel, chunk=chunk),
        out_shape=jax.ShapeDtypeStruct((cap * SUBLANES, LANES), F32),
        grid_spec=grid_spec,
        compiler_params=pltpu.CompilerParams(dimension_semantics=("arbitrary",),
                                             vmem_limit_bytes=MOE_VMEM_LIMIT),
        name="moe_experts",
    )(block_e, nvalid, rows, x1, w_gu, b_gu.reshape(depth, n_exp, 1, 2 * de),
      w_dn, b_dn.reshape(depth, n_exp, 1, d))


def _combine_kernel(pos_hbm, y_hbm, x_ref, gate_ref, g_ref, b_ref, o_ref,
                    buf_ref, pos_smem, pos_sem, row_sem, *, tm, alpha):
    slot = _pipelined_gather(pos_hbm, y_hbm, buf_ref, pos_smem, pos_sem, row_sem, TOP_K * tm,
                             lambda step: step >= 0)
    gates = gate_ref[...]
    z = alpha * _from_tiles(x_ref, tm)
    for kk in range(TOP_K):
        z = z + gates[:, kk:kk + 1] * _from_tiles(buf_ref.at[slot], tm, base=kk * tm * SUBLANES)
    o_ref[...] = _layer_norm(z, g_ref[...], b_ref[...])


def _combine_norm(y, pos, x1, gates, ln_g, ln_b, alpha, tm=256):
    d = SUBLANES * LANES
    n = x1.shape[0] // SUBLANES
    steps = n // tm
    pos_t = pos.reshape(steps, tm, TOP_K).transpose(0, 2, 1).reshape(steps, TOP_K * tm)
    row = lambda i: (i, 0)
    fixed = lambda i: (0, 0)
    return pl.pallas_call(
        functools.partial(_combine_kernel, tm=tm, alpha=alpha),
        out_shape=jax.ShapeDtypeStruct((n, d), F32),
        grid=(steps,),
        in_specs=[
            pl.BlockSpec(memory_space=pl.ANY),
            pl.BlockSpec(memory_space=pl.ANY),
            pl.BlockSpec((tm * SUBLANES, LANES), row),
            pl.BlockSpec((tm, 8), row),
            pl.BlockSpec((1, d), fixed),
            pl.BlockSpec((1, d), fixed),
        ],
        out_specs=pl.BlockSpec((tm, d), row),
        scratch_shapes=[pltpu.VMEM((2, TOP_K * tm * SUBLANES, LANES), F32),
                        pltpu.SMEM((2 * TOP_K * tm,), jnp.int32),
                        pltpu.SemaphoreType.DMA((2,)),
                        pltpu.SemaphoreType.DMA((2,))],
        compiler_params=_cparams("arbitrary"),
        name="combine_norm",
    )(pos_t, y, x1, gates, ln_g.reshape(1, d), ln_b.reshape(1, d))


def _routing_tables(top_idx, n):
    nk = n * TOP_K
    flat_e = top_idx[:, :TOP_K].reshape(nk)
    assign = jnp.arange(nk, dtype=jnp.int32)
    skey = lax.sort(flat_e * nk + assign)
    order = skey % nk
    experts = jnp.arange(N_EXPERTS + 1, dtype=jnp.int32)
    bounds = jnp.sum(((skey // nk)[None, :] < experts[:, None]).astype(jnp.int32), axis=1)
    start, counts = bounds[:-1], bounds[1:] - bounds[:-1]
    _, sorted_at = lax.sort((order, assign), num_keys=1)
    padded = (counts + MOE_BLOCK - 1) // MOE_BLOCK * MOE_BLOCK
    pad_end = jnp.cumsum(padded)
    pad_start = pad_end - padded
    pos = (pad_start[flat_e] + sorted_at - start[flat_e]).astype(jnp.int32)
    n_blocks = nk // MOE_BLOCK + N_EXPERTS + 1
    blk_first = jnp.arange(n_blocks, dtype=jnp.int32) * MOE_BLOCK
    block_e = jnp.sum((pad_end[None, :] <= blk_first[:, None]).astype(jnp.int32), axis=1)
    block_e = jnp.minimum(block_e, N_EXPERTS - 1)
    nvalid = jnp.clip(counts[block_e] - (blk_first - pad_start[block_e]), 0, MOE_BLOCK)
    nvalid = jnp.where(blk_first < pad_end[-1], nvalid, 0).astype(jnp.int32)
    slot = jnp.arange(n_blocks * MOE_BLOCK, dtype=jnp.int32)
    slot_e = jnp.repeat(block_e, MOE_BLOCK)
    within = slot - pad_start[slot_e]
    live = (within < counts[slot_e]) & (slot < pad_end[-1])
    src = order[jnp.clip(start[slot_e] + within, 0, nk - 1)] // TOP_K
    rows = jnp.where(live, src, 0).astype(jnp.int32)
    return rows.reshape(n_blocks, MOE_BLOCK), pos.reshape(n, TOP_K), block_e, nvalid


_EVEN_PLAN = (
    (512, "attn", HEAD_DIM ** -0.5), (128, "attn", 1.0), (128, None, 1.0),
    (512, None, 1.0), (512, None, 1.0), (512, None, 1.0), (512, None, 1.0),
)
_ODD_PLAN = (
    (256, "ret", 1.0), (256, "ret", C_KEY_DIM ** -0.5), (512, None, 1.0), (512, None, 1.0),
) + ((512, "attn", HEAD_DIM ** -0.5), (128, "attn", 1.0), (128, None, 1.0)) * len(D_PATTERNS)


def kernel(x, w_in_even, b_in_even, attn_sinks, hgrn_lb_logits, hgrn_norm, w_out_even, w_in_odd,
           b_in_odd, ret_norm, w_out_odd, ln1_g, ln1_b, ln2_g, ln2_b, router_w, router_b,
           expert_w_gu, expert_b_gu, expert_w_dn, expert_b_dn):
    batch, seq, d = x.shape
    n = batch * seq
    depth = ln1_g.shape[0]
    alpha = float((2 * depth) ** 0.25)

    rope_inv = 1.0 / (ROPE_THETA ** (jnp.arange(0, ROPE_DIM, 2, dtype=F32) / ROPE_DIM))
    ret_inv = 1.0 / (RET_THETA ** jnp.linspace(0.0, 1.0, C_KEY_DIM // 2, dtype=F32))
    tab_attn = _rope_tables(seq, ROPE_DIM // 2, rope_inv, HEAD_DIM)
    tab_ret = _rope_tables(seq, C_KEY_DIM // 2, ret_inv, C_KEY_DIM)
    lb_soft = jax.nn.softmax(hgrn_lb_logits.astype(F32), axis=0)
    lower_bounds = jnp.concatenate(
        [jnp.zeros_like(lb_soft[:1]), jnp.cumsum(lb_soft, axis=0)[:-1]], axis=0)

    xf = x.reshape(n, d)
    for layer in range(depth):
        j = layer // 2
        if layer % 2 == 0:
            aq, ak, av, bq, bf, bi, bg = _inproj(
                xf, w_in_even[j].astype(BF16), b_in_even[j].reshape(1, -1), tab_attn, tab_ret,
                _EVEN_PLAN, seq)
            (ya,) = _banded_attention(aq, ak, av, batch, seq, 1, A_WINDOW - 1,
                                      sinks=attn_sinks[j])
            yb = _hgrn2(bq, bf, bi, bg, lower_bounds[j], hgrn_norm[j], batch, seq)
            w_out = w_out_even[j]
        else:
            outs = _inproj(xf, w_in_odd[j].astype(BF16), b_in_odd[j].reshape(1, -1), tab_attn,
                           tab_ret, _ODD_PLAN, seq)
            cq, ck, cv, cg = outs[:4]
            ya = _retention(cq, ck, cv, cg, ret_norm[j], batch, seq)
            parts = []
            for p, (window, dil) in enumerate(D_PATTERNS):
                dq, dk, dv = outs[4 + 3 * p:7 + 3 * p]
                o, lse = _banded_attention(dq, dk, dv, batch, seq, dil, window // dil,
                                           want_lse=True, out_dtype=F32)
                parts += [o, lse]
            yb = _merge_patterns(parts)
            w_out = w_out_odd[j]
        x1, top_idx, gates = _outproj_norm_route(
            ya, yb, w_out, xf, ln1_g[layer], ln1_b[layer], router_w[layer], router_b[layer], alpha)
        rows, pos, block_e, nvalid = _routing_tables(top_idx, n)
        y = _moe_experts(x1, rows, block_e, nvalid, expert_w_gu, expert_b_gu, expert_w_dn,
                         expert_b_dn, layer)
        xf = _combine_norm(y, pos, x1, gates, ln2_g[layer], ln2_b[layer], alpha)
    return xf.reshape(batch, seq, d)
```

```python
import functools
import math

import jax
import jax.numpy as jnp
from jax import lax
from jax.experimental import pallas as pl
from jax.experimental.pallas import tpu as pltpu

F32 = jnp.float32
BF16 = jnp.bfloat16

HEAD_DIM = 64
ATTN_BLOCK = 128
A_Q_HEADS = 8
A_KV_HEADS = 2
A_WINDOW = 128
B_HEADS = 4
B_KEY_DIM = 128
B_MIN_F = 1e-30
C_HEADS = 4
C_KEY_DIM = 64
C_VAL_DIM = 128
D_PATTERNS = ((128, 1), (512, 4), (2048, 16))
ROPE_THETA = 500000.0
ROPE_DIM = HEAD_DIM // 4
RET_THETA = 10000.0
N_EXPERTS = 32
TOP_K = 4
SWIGLU_LIMIT = 7.0
SWIGLU_ALPHA = 1.702
MOE_BLOCK = 512
LN_EPS = 1e-5
NORM_EPS = 1e-6
NEG_BIG = -1e30

LANES = 128
SUBLANES = 8
VMEM_LIMIT = 48 * 1024 * 1024
MOE_VMEM_LIMIT = 56 * 1024 * 1024

HGRN_CHUNK = 128
HGRN_SUB = 8
HGRN_LEVELS = (16, 32, 64, 128)


def _cparams(*sem):
    return pltpu.CompilerParams(dimension_semantics=sem, vmem_limit_bytes=VMEM_LIMIT)


def _to_tiles(ref, value):
    rows = value.shape[0]
    for c in range(SUBLANES):
        ref[pl.ds(c, rows, stride=SUBLANES), :] = value[:, c * LANES:(c + 1) * LANES]


def _from_tiles(ref, rows, base=0):
    return jnp.concatenate(
        [ref[pl.ds(base + c, rows, stride=SUBLANES), :] for c in range(SUBLANES)], axis=1)


def _sigmoid(z):
    return 1.0 / (1.0 + jnp.exp(-z))


def _rotate(h, cos, sa, sb, shift):
    width = h.shape[1]
    rep = width // LANES
    if rep > 1:
        cos = jnp.concatenate([cos] * rep, axis=1)
        sa = jnp.concatenate([sa] * rep, axis=1)
        sb = jnp.concatenate([sb] * rep, axis=1)
    up = pltpu.roll(h, width - shift, axis=1)
    dn = pltpu.roll(h, shift, axis=1)
    return h * cos + up * sa + dn * sb


def _inproj_kernel(x_ref, w_ref, b_ref, ta_ref, tr_ref, *out_refs, plan):
    xb = x_ref[...].astype(BF16)
    off = 0
    for o_ref, (width, rope, scale) in zip(out_refs, plan):
        h = jnp.dot(xb, w_ref[:, off:off + width], preferred_element_type=F32)
        h = h + b_ref[:, off:off + width]
        if rope == "attn":
            h = _rotate(h, ta_ref[0], ta_ref[1], ta_ref[2], ROPE_DIM // 2)
        elif rope == "ret":
            h = _rotate(h, tr_ref[0], tr_ref[1], tr_ref[2], C_KEY_DIM // 2)
        if scale != 1.0:
            h = h * scale
        o_ref[...] = h.astype(o_ref.dtype)
        off += width


def _inproj(x, w, b, tab_attn, tab_ret, plan, seq, tm=512):
    n, d = x.shape
    cols = w.shape[1]
    tpb = seq // tm
    kern = functools.partial(_inproj_kernel, plan=plan)
    outs = tuple(jax.ShapeDtypeStruct((n, wd), F32) for wd, _, _ in plan)
    return pl.pallas_call(
        kern,
        out_shape=outs,
        grid=(n // tm,),
        in_specs=[
            pl.BlockSpec((tm, d), lambda i: (i, 0)),
            pl.BlockSpec((d, cols), lambda i: (0, 0)),
            pl.BlockSpec((1, cols), lambda i: (0, 0)),
            pl.BlockSpec((3, tm, LANES), lambda i: (0, i % tpb, 0)),
            pl.BlockSpec((3, tm, LANES), lambda i: (0, i % tpb, 0)),
        ],
        out_specs=tuple(pl.BlockSpec((tm, wd), lambda i: (i, 0)) for wd, _, _ in plan),
        compiler_params=_cparams("parallel"),
        name="inproj",
    )(x, w, b, tab_attn, tab_ret)


def _rope_tables(seq, half, inv_freq, period):
    pos = jnp.arange(seq, dtype=F32)
    ang = pos[:, None] * inv_freq[None, :]
    cos, sin = jnp.cos(ang), jnp.sin(ang)
    pad = period - 2 * half
    ones = jnp.ones((seq, pad), F32)
    zeros = jnp.zeros((seq, pad), F32)
    zh = jnp.zeros((seq, half), F32)
    cosf = jnp.concatenate([cos, cos, ones], axis=1)
    sa = jnp.concatenate([-sin, zh, zeros], axis=1)
    sb = jnp.concatenate([zh, sin, zeros], axis=1)
    rep = LANES // period
    return jnp.stack([jnp.tile(t, (1, rep)) for t in (cosf, sa, sb)], axis=0)


def _attn_kernel(*refs, dil, max_dist, nsub, has_sink, want_lse):
    if has_sink:
        sink_ref, refs = refs[0], refs[1:]
    q_refs = refs[:4]
    k_ref, v_ref, kp_ref, vp_ref = refs[4:8]
    o_ref = refs[8]
    lse_ref = refs[9] if want_lse else None
    n_out = 2 if want_lse else 1
    stage = refs[8 + n_out:]
    blk = ATTN_BLOCK
    qi = lax.broadcasted_iota(jnp.int32, (4 * blk, 2 * blk), 0) % blk
    kj = lax.broadcasted_iota(jnp.int32, (4 * blk, 2 * blk), 1)
    dist = qi - kj + blk
    band = (dist >= 0) & (dist <= max_dist)
    first_kmin = jnp.where(pl.program_id(1) == 0, blk, 0)
    band_first = band & (kj >= first_kmin)
    row4 = lax.broadcasted_iota(jnp.int32, (4 * blk, 1), 0) // blk

    def take(start):
        return pl.ds(start, blk, stride=dil) if dil > 1 else pl.ds(start, blk)

    def one_group(r, j):
        rows = take(j * blk * dil + r)
        if j == 0:
            k_prev, v_prev = kp_ref[take(r), :], vp_ref[take(r), :]
            valid = band_first
        else:
            prow = take((j - 1) * blk * dil + r)
            k_prev, v_prev = k_ref[prow, :], v_ref[prow, :]
            valid = band
        k_cat = jnp.concatenate([k_prev, k_ref[rows, :]], axis=0).astype(BF16)
        v_cat = jnp.concatenate([v_prev, v_ref[rows, :]], axis=0).astype(BF16)
        for h in range(A_KV_HEADS):
            qa, qb = q_refs[2 * h][rows, :], q_refs[2 * h + 1][rows, :]
            q4 = jnp.concatenate([qa[:, :HEAD_DIM], qa[:, HEAD_DIM:], qb[:, :HEAD_DIM],
                                  qb[:, HEAD_DIM:]], axis=0).astype(BF16)
            kh = k_cat[:, h * HEAD_DIM:(h + 1) * HEAD_DIM]
            vh = v_cat[:, h * HEAD_DIM:(h + 1) * HEAD_DIM]
            s = lax.dot_general(q4, kh, (((1,), (1,)), ((), ())), preferred_element_type=F32)
            s = jnp.where(valid, s, NEG_BIG)
            m = jnp.max(s, axis=1, keepdims=True)
            if has_sink:
                sk = jnp.full((4 * blk, 1), sink_ref[4 * h + 3], F32)
                for g in range(3):
                    sk = jnp.where(row4 == g, sink_ref[4 * h + g], sk)
                m = jnp.maximum(m, sk)
            p = jnp.exp(s - m)
            den = jnp.sum(p, axis=1, keepdims=True)
            if has_sink:
                den = den + jnp.exp(sk - m)
            o = jnp.dot(p.astype(BF16), vh, preferred_element_type=F32) / den
            lse = m + jnp.log(den) if want_lse else None
            for half in range(2):
                cb = 2 * h + half
                g0 = 2 * half
                pair = jnp.concatenate(
                    [o[g0 * blk:(g0 + 1) * blk], o[(g0 + 1) * blk:(g0 + 2) * blk]], axis=1)
                if dil > 1:
                    stage[0][cb, rows, :] = pair
                else:
                    o_ref[rows, cb * LANES:(cb + 1) * LANES] = pair.astype(o_ref.dtype)
                if want_lse:
                    lpair = jnp.concatenate(
                        [jnp.broadcast_to(lse[g * blk:(g + 1) * blk], (blk, HEAD_DIM))
                         for g in (g0, g0 + 1)], axis=1)
                    if dil > 1:
                        stage[1][cb, rows, :] = lpair
                    else:
                        lse_ref[rows, cb * LANES:(cb + 1) * LANES] = lpair

    for j in range(nsub):
        if dil == 1:
            one_group(0, j)
        else:
            def body(r, carry, j=j):
                one_group(r, j)
                return carry
            lax.fori_loop(0, dil, body, 0)
    if dil > 1:
        for cb in range(4):
            o_ref[:, cb * LANES:(cb + 1) * LANES] = stage[0][cb].astype(o_ref.dtype)
            if want_lse:
                lse_ref[:, cb * LANES:(cb + 1) * LANES] = stage[1][cb]


def _banded_attention(q, k, v, batch, seq, dil, max_dist, sinks=None, want_lse=False,
                      out_dtype=BF16):
    span = ATTN_BLOCK * dil
    nsub = max(1, 512 // span)
    tb = span * nsub
    nt = seq // tb
    qw, kw = q.shape[1], k.shape[1]
    q3, k3, v3 = (t.reshape(batch, seq, t.shape[1]) for t in (q, k, v))
    kern = functools.partial(_attn_kernel, dil=dil, max_dist=max_dist, nsub=nsub,
                             has_sink=sinks is not None, want_lse=want_lse)
    cur = lambda b, t: (b, t, 0)
    prev = lambda b, t: (b, jnp.maximum(t * nsub - 1, 0), 0)
    in_specs = [pl.BlockSpec((None, tb, LANES), lambda b, t, c=c: (b, t, c)) for c in range(4)] + [
        pl.BlockSpec((None, tb, kw), cur),
        pl.BlockSpec((None, tb, kw), cur),
        pl.BlockSpec((None, span, kw), prev),
        pl.BlockSpec((None, span, kw), prev),
    ]
    args = [q3, q3, q3, q3, k3, v3, k3, v3]
    if sinks is not None:
        in_specs = [pl.BlockSpec(memory_space=pltpu.SMEM)] + in_specs
        args = [sinks.astype(F32)] + args
    n_out = 2 if want_lse else 1
    out_shape = [jax.ShapeDtypeStruct((batch, seq, qw), out_dtype)]
    out_specs = [pl.BlockSpec((None, tb, qw), cur)]
    if want_lse:
        out_shape.append(jax.ShapeDtypeStruct((batch, seq, qw), F32))
        out_specs.append(pl.BlockSpec((None, tb, qw), cur))
    scratch = [pltpu.VMEM((4, tb, LANES), F32)] * n_out if dil > 1 else []
    res = pl.pallas_call(
        kern,
        out_shape=tuple(out_shape),
        grid=(batch, nt),
        in_specs=in_specs,
        out_specs=tuple(out_specs),
        scratch_shapes=scratch,
        compiler_params=_cparams("parallel", "arbitrary"),
        name=f"banded_attn_d{dil}",
    )(*args)
    return tuple(r.reshape(batch * seq, qw) for r in res)


def _merge_kernel(o1, l1, o2, l2, o3, l3, out_ref):
    a, b, c = l1[...], l2[...], l3[...]
    m = jnp.maximum(jnp.maximum(a, b), c)
    ea, eb, ec = jnp.exp(a - m), jnp.exp(b - m), jnp.exp(c - m)
    num = ea * o1[...] + eb * o2[...] + ec * o3[...]
    out_ref[...] = (num / (ea + eb + ec)).astype(out_ref.dtype)


def _merge_patterns(parts, tm=512):
    n, w = parts[0].shape
    spec = pl.BlockSpec((tm, w), lambda i: (i, 0))
    return pl.pallas_call(
        _merge_kernel,
        out_shape=jax.ShapeDtypeStruct((n, w), BF16),
        grid=(n // tm,),
        in_specs=[spec] * 6,
        out_specs=spec,
        compiler_params=_cparams("parallel"),
        name="merge_patterns",
    )(*parts)


def _hgrn_structure():
    c = HGRN_CHUNK
    r = lax.broadcasted_iota(jnp.int32, (c, c), 0)
    j = lax.broadcasted_iota(jnp.int32, (c, c), 1)
    blocks = [(j <= r)]
    for lv in HGRN_LEVELS:
        mid = (r // lv) * lv + lv // 2 - 1
        second = (r % lv) >= lv // 2
        lo = jnp.where(second, mid, r)
        hi = jnp.where(second, r, mid)
        blocks.append((j > lo) & (j <= hi))
    return jnp.concatenate([jnp.where(m, 1.0, 0.0) for m in blocks], axis=0).astype(BF16)


def _split3(x):
    hi = x.astype(BF16)
    r1 = x - hi.astype(F32)
    mid = r1.astype(BF16)
    lo = (r1 - mid.astype(F32)).astype(BF16)
    return hi, mid, lo


def _hgrn_kernel(q_ref, f_ref, i_ref, g_ref, lb_ref, gain_ref, o_ref, st_ref, *, nchunk):
    c = HGRN_CHUNK

    @pl.when(pl.program_id(2) == 0)
    def _():
        st_ref[...] = jnp.zeros_like(st_ref)

    dmat = _hgrn_structure()
    row = lax.broadcasted_iota(jnp.int32, (c, c), 0)
    col = lax.broadcasted_iota(jnp.int32, (c, c), 1)
    lb = jnp.clip(lb_ref[...], 0.0, 1.0)
    gain = gain_ref[...]
    sub_row = lax.broadcasted_iota(jnp.int32, (HGRN_SUB, c), 0)
    sub_col = lax.broadcasted_iota(jnp.int32, (HGRN_SUB, c), 1)

    for ci in range(nchunk):
        rows = pl.ds(ci * c, c)
        q = q_ref[rows, :]
        f = lb + (1.0 - lb) * _sigmoid(f_ref[rows, :])
        lf = jnp.log(jnp.maximum(f, B_MIN_F))
        key = 1.0 - f
        val = i_ref[rows, :].astype(BF16)

        hi, mid, lo = _split3(lf)
        sums = (jnp.dot(dmat, hi, preferred_element_type=F32)
                + jnp.dot(dmat, mid, preferred_element_type=F32)
                + jnp.dot(dmat, lo, preferred_element_type=F32))
        b = sums[0:c]

        att = jnp.zeros((c, c), F32)
        for n, lv in enumerate(HGRN_LEVELS):
            e = jnp.exp(sums[(n + 1) * c:(n + 2) * c])
            second = (row % lv) >= lv // 2
            qs = jnp.where(second, q * e, 0.0).astype(BF16)
            ks = jnp.where(second, 0.0, key * e).astype(BF16)
            a = lax.dot_general(qs, ks, (((1,), (1,)), ((), ())), preferred_element_type=F32)
            att = att + jnp.where((row // lv) == (col // lv), a, 0.0)

        diag = []
        for a0 in range(0, c, HGRN_SUB):
            qa, ka, ba = q[a0:a0 + HGRN_SUB], key[a0:a0 + HGRN_SUB], b[a0:a0 + HGRN_SUB]
            acc = jnp.zeros((HGRN_SUB, c), F32)
            for s in range(HGRN_SUB):
                e = jnp.exp(jnp.minimum(ba - ba[s:s + 1, :], 0.0))
                w = jnp.sum(qa * e * ka[s:s + 1, :], axis=1, keepdims=True)
                acc = jnp.where((sub_col == a0 + s) & (sub_row >= s), w, acc)
            diag.append(acc)
        att = att + jnp.concatenate(diag, axis=0)

        st = st_ref[...]
        b_last = b[c - 1:c, :]
        q_in = (q * jnp.exp(b)).astype(BF16)
        k_out = (key * jnp.exp(b_last - b)).astype(BF16)
        o = jnp.dot(att.astype(BF16), val, preferred_element_type=F32)
        o = o + lax.dot_general(q_in, st.astype(BF16), (((1,), (1,)), ((), ())),
                                preferred_element_type=F32)
        st_ref[...] = jnp.exp(b_last) * st + lax.dot_general(
            val, k_out, (((0,), (0,)), ((), ())), preferred_element_type=F32)

        o = o * lax.rsqrt(jnp.mean(o * o, axis=1, keepdims=True) + NORM_EPS)
        g = g_ref[rows, :]
        o_ref[rows, :] = (o * gain * (g * _sigmoid(g))).astype(o_ref.dtype)


def _hgrn2(q, f, i_in, g, lb, gain, batch, seq, nchunk=4):
    width = q.shape[1]
    heads = width // B_KEY_DIM
    tb = HGRN_CHUNK * nchunk
    q3, f3, i3, g3 = (t.reshape(batch, seq, width) for t in (q, f, i_in, g))
    blk = pl.BlockSpec((None, tb, B_KEY_DIM), lambda b, h, t: (b, t, h))
    vec = pl.BlockSpec((1, B_KEY_DIM), lambda b, h, t: (0, h))
    out = pl.pallas_call(
        functools.partial(_hgrn_kernel, nchunk=nchunk),
        out_shape=jax.ShapeDtypeStruct((batch, seq, width), BF16),
        grid=(batch, heads, seq // tb),
        in_specs=[blk, blk, blk, blk, vec, vec],
        out_specs=blk,
        scratch_shapes=[pltpu.VMEM((B_KEY_DIM, B_KEY_DIM), F32)],
        compiler_params=_cparams("parallel", "parallel", "arbitrary"),
        name="hgrn2",
    )(q3, f3, i3, g3, lb.reshape(1, width).astype(F32), gain.reshape(1, width).astype(F32))
    return out.reshape(batch * seq, width)


def _retention_kernel(q_ref, k_ref, v_ref, g_ref, gain_ref, o_ref, st_ref, *, nchunk):
    c = ATTN_BLOCK

    @pl.when(pl.program_id(1) == 0)
    def _():
        st_ref[...] = jnp.zeros_like(st_ref)

    ri = lax.broadcasted_iota(jnp.int32, (c, c), 0)
    ci_ = lax.broadcasted_iota(jnp.int32, (c, c), 1)
    rel = (ri - ci_).astype(F32)
    rowf = ri.astype(F32)

    for h in range(C_HEADS):
        log_gamma = math.log1p(-(2.0 ** (-5.0 - h)))
        decay = jnp.where(rel >= 0.0, jnp.exp(log_gamma * jnp.maximum(rel, 0.0)), 0.0)
        head_scale = jnp.exp(log_gamma * (rowf + 1.0))
        tail_scale = jnp.exp(log_gamma * (c - 1.0 - rowf))
        chunk_decay = math.exp(log_gamma * c)
        gain = gain_ref[:, h * C_VAL_DIM:(h + 1) * C_VAL_DIM]
        for ci in range(nchunk):
            rows = pl.ds(ci * c, c)
            q = q_ref[rows, h * C_KEY_DIM:(h + 1) * C_KEY_DIM].astype(BF16)
            k = k_ref[rows, h * C_KEY_DIM:(h + 1) * C_KEY_DIM].astype(BF16)
            v = v_ref[rows, h * C_VAL_DIM:(h + 1) * C_VAL_DIM]
            st = st_ref[h]
            s = lax.dot_general(q, k, (((1,), (1,)), ((), ())), preferred_element_type=F32)
            o = jnp.dot((s * decay).astype(BF16), v.astype(BF16), preferred_element_type=F32)
            o = o + head_scale * jnp.dot(q, st.astype(BF16), preferred_element_type=F32)
            st_ref[h] = chunk_decay * st + lax.dot_general(
                k, (v * tail_scale).astype(BF16), (((0,), (0,)), ((), ())),
                preferred_element_type=F32)
            o = o - jnp.mean(o, axis=1, keepdims=True)
            o = o * lax.rsqrt(jnp.mean(o * o, axis=1, keepdims=True) + NORM_EPS)
            g = g_ref[rows, h * C_VAL_DIM:(h + 1) * C_VAL_DIM]
            o_ref[rows, h * C_VAL_DIM:(h + 1) * C_VAL_DIM] = (
                o * gain * (g * _sigmoid(g))).astype(o_ref.dtype)


def _retention(q, k, v, g, gain, batch, seq, nchunk=4):
    tb = ATTN_BLOCK * nchunk
    kw, vw = q.shape[1], v.shape[1]
    q3, k3 = q.reshape(batch, seq, kw), k.reshape(batch, seq, kw)
    v3, g3 = v.reshape(batch, seq, vw), g.reshape(batch, seq, vw)
    kspec = pl.BlockSpec((None, tb, kw), lambda b, t: (b, t, 0))
    vspec = pl.BlockSpec((None, tb, vw), lambda b, t: (b, t, 0))
    out = pl.pallas_call(
        functools.partial(_retention_kernel, nchunk=nchunk),
        out_shape=jax.ShapeDtypeStruct((batch, seq, vw), BF16),
        grid=(batch, seq // tb),
        in_specs=[kspec, kspec, vspec, vspec, pl.BlockSpec((1, vw), lambda b, t: (0, 0))],
        out_specs=vspec,
        scratch_shapes=[pltpu.VMEM((C_HEADS, C_KEY_DIM, C_VAL_DIM), F32)],
        compiler_params=_cparams("parallel", "arbitrary"),
        name="retention",
    )(q3, k3, v3, g3, gain.reshape(1, vw).astype(F32))
    return out.reshape(batch * seq, vw)


def _layer_norm(z, g, b):
    mu = jnp.mean(z, axis=1, keepdims=True)
    zc = z - mu
    var = jnp.mean(zc * zc, axis=1, keepdims=True)
    return zc * lax.rsqrt(var + LN_EPS) * g + b


def _outproj_kernel(ya_ref, yb_ref, w_ref, x_ref, g_ref, b_ref, rwh_ref, rwl_ref, rb_ref,
                    x1_ref, idx_ref, gate_ref, hist_ref, *, alpha):
    half = ya_ref.shape[1]
    mix = jnp.dot(ya_ref[...], w_ref[0:half, :], preferred_element_type=F32)
    mix = mix + jnp.dot(yb_ref[...], w_ref[half:, :], preferred_element_type=F32)
    x1 = _layer_norm(alpha * x_ref[...] + mix, g_ref[...], b_ref[...])
    _to_tiles(x1_ref, x1)

    xh = x1.astype(BF16)
    xl = (x1 - xh.astype(F32)).astype(BF16)
    logits = (jnp.dot(xh, rwh_ref[...], preferred_element_type=F32)
              + jnp.dot(xh, rwl_ref[...], preferred_element_type=F32)
              + jnp.dot(xl, rwh_ref[...], preferred_element_type=F32)) + rb_ref[...]
    lane = lax.broadcasted_iota(jnp.int32, logits.shape, 1)
    idx_acc = jnp.zeros(logits.shape, jnp.int32)
    val_acc = jnp.zeros(logits.shape, F32)
    top0 = None
    den = None
    for kk in range(TOP_K):
        m = jnp.max(logits, axis=1, keepdims=True)
        sel = jnp.min(jnp.where(logits == m, lane, LANES), axis=1, keepdims=True)
        if kk == 0:
            top0 = m
        e = jnp.exp(m - top0)
        den = e if den is None else den + e
        idx_acc = jnp.where(lane == kk, sel, idx_acc)
        val_acc = jnp.where(lane == kk, e, val_acc)
        picked = lane == sel
        hits = jnp.sum(jnp.where(picked, 1.0, 0.0), axis=0, keepdims=True)
        hist = hits if kk == 0 else hist + hits
        logits = jnp.where(picked, NEG_BIG * 2.0, logits)
    idx_ref[...] = idx_acc[:, 0:idx_ref.shape[1]]
    gate_ref[...] = (val_acc / den)[:, 0:gate_ref.shape[1]]
    hist_ref[...] = jnp.broadcast_to(hist, hist_ref.shape).astype(jnp.int32)


def _outproj_norm_route(ya, yb, w_out, x, ln_g, ln_b, rw, rb, alpha, tm=512):
    n, d = x.shape
    half = ya.shape[1]
    rw_pad = jnp.zeros((d, LANES), F32).at[:, :N_EXPERTS].set(rw.astype(F32))
    rwh = rw_pad.astype(BF16)
    rwl = (rw_pad - rwh.astype(F32)).astype(BF16)
    rb_pad = jnp.full((1, LANES), NEG_BIG, F32).at[0, :N_EXPERTS].set(rb.astype(F32))
    row = lambda i: (i, 0)
    fixed = lambda i: (0, 0)
    return pl.pallas_call(
        functools.partial(_outproj_kernel, alpha=alpha),
        out_shape=(jax.ShapeDtypeStruct((n * SUBLANES, LANES), F32),
                   jax.ShapeDtypeStruct((n, 8), jnp.int32),
                   jax.ShapeDtypeStruct((n, 8), F32),
                   jax.ShapeDtypeStruct((n // tm * SUBLANES, LANES), jnp.int32)),
        grid=(n // tm,),
        in_specs=[
            pl.BlockSpec((tm, half), row),
            pl.BlockSpec((tm, half), row),
            pl.BlockSpec((2 * half, d), fixed),
            pl.BlockSpec((tm, d), row),
            pl.BlockSpec((1, d), fixed),
            pl.BlockSpec((1, d), fixed),
            pl.BlockSpec((d, LANES), fixed),
            pl.BlockSpec((d, LANES), fixed),
            pl.BlockSpec((1, LANES), fixed),
        ],
        out_specs=(pl.BlockSpec((tm * SUBLANES, LANES), row),
                   pl.BlockSpec((tm, 8), row),
                   pl.BlockSpec((tm, 8), row),
                   pl.BlockSpec((SUBLANES, LANES), row)),
        compiler_params=_cparams("parallel"),
        name="outproj_norm_route",
    )(ya, yb, w_out.astype(BF16), x, ln_g.reshape(1, d), ln_b.reshape(1, d), rwh, rwl, rb_pad)


ISSUE_UNROLL = 8
DRAIN_UNROLL = 16


def _issue_tiles(idx_smem, base, src_hbm, dst_ref, sem, count):
    def body(g, carry):
        for u in range(ISSUE_UNROLL):
            r = g * ISSUE_UNROLL + u
            src = pl.multiple_of(idx_smem[base + r] * SUBLANES, SUBLANES)
            dst = pl.multiple_of(r * SUBLANES, SUBLANES)
            pltpu.make_async_copy(src_hbm.at[pl.ds(src, SUBLANES), :],
                                  dst_ref.at[pl.ds(dst, SUBLANES), :], sem).start(priority=u % 2)
        return carry

    lax.fori_loop(0, count // ISSUE_UNROLL, body, 0)


def _drain_tiles(src_hbm, dst_ref, sem, count):
    def body(r, carry):
        dst = pl.multiple_of(r * SUBLANES, SUBLANES)
        pltpu.make_async_copy(src_hbm.at[pl.ds(0, SUBLANES), :],
                              dst_ref.at[pl.ds(dst, SUBLANES), :], sem).wait()
        return carry

    def group(g, carry):
        for u in range(DRAIN_UNROLL):
            body(g * DRAIN_UNROLL + u, carry)
        return carry

    lax.fori_loop(0, count // DRAIN_UNROLL, group, 0)


def _pipelined_gather(idx_hbm, src_hbm, buf, idx_smem, idx_sem, row_sem, count, rows_of):
    i = pl.program_id(0)
    n = pl.num_programs(0)
    slot = i % 2
    nslot = 1 - slot

    def idx_copy(step, s):
        return pltpu.make_async_copy(idx_hbm.at[step], idx_smem.at[pl.ds(s * count, count)],
                                     idx_sem.at[s])

    def issue(step, s):
        _issue_tiles(idx_smem, s * count, src_hbm, buf.at[s], row_sem.at[s], rows_of(step))

    @pl.when(i == 0)
    def _():
        idx_copy(0, 0).start()
        idx_copy(0, 0).wait()
        issue(0, 0)

        @pl.when(n > 1)
        def _():
            idx_copy(1, 1).start()

    @pl.when(i + 1 < n)
    def _():
        idx_copy(i + 1, nslot).wait()
        issue(i + 1, nslot)

    @pl.when(i + 2 < n)
    def _():
        idx_copy(i + 2, slot).start()

    _drain_tiles(src_hbm, buf.at[slot], row_sem.at[slot], rows_of(i))
    return slot


def _moe_kernel(be_ref, nv_ref, rows_hbm, x_hbm, wgu_ref, bgu_ref, wdn_ref, bdn_ref, y_ref,
                xbuf, idx_smem, idx_sem, row_sem, act_ref, wgu_bf, wdn_bf, *, chunk):
    i = pl.program_id(0)
    de = wdn_ref.shape[0]
    @pl.when(i == 0)
    def _():
        xbuf[...] = jnp.zeros_like(xbuf)

    slot = _pipelined_gather(
        rows_hbm, x_hbm, xbuf, idx_smem, idx_sem, row_sem, MOE_BLOCK,
        lambda step: (nv_ref[step] + DRAIN_UNROLL - 1) // DRAIN_UNROLL * DRAIN_UNROLL)

    @pl.when((i == 0) | (be_ref[i] != be_ref[jnp.maximum(i - 1, 0)]))
    def _():
        for c0 in range(0, 2 * de, chunk):
            wgu_bf[:, c0:c0 + chunk] = wgu_ref[:, c0:c0 + chunk].astype(BF16)
        for c0 in range(0, wdn_ref.shape[1], chunk):
            wdn_bf[:, c0:c0 + chunk] = wdn_ref[:, c0:c0 + chunk].astype(BF16)

    @pl.when(nv_ref[i] > 0)
    def _():
        xb = _from_tiles(xbuf.at[slot], MOE_BLOCK).astype(BF16)
        for c0 in range(0, de, chunk):
            gate = jnp.dot(xb, wgu_bf[:, c0:c0 + chunk], preferred_element_type=F32)
            gate = gate + bgu_ref[:, c0:c0 + chunk]
            up = jnp.dot(xb, wgu_bf[:, de + c0:de + c0 + chunk], preferred_element_type=F32)
            up = up + bgu_ref[:, de + c0:de + c0 + chunk]
            gate = jnp.minimum(gate, SWIGLU_LIMIT)
            up = jnp.clip(up, -SWIGLU_LIMIT, SWIGLU_LIMIT)
            act = (up + 1.0) * gate * _sigmoid(SWIGLU_ALPHA * gate)
            act_ref[:, c0:c0 + chunk] = act.astype(BF16)
        _to_tiles(y_ref, jnp.dot(act_ref[...], wdn_bf[...], preferred_element_type=F32)
                  + bdn_ref[...])

    @pl.when(nv_ref[i] == 0)
    def _():
        y_ref[...] = jnp.zeros_like(y_ref)


def _moe_experts(x1, rows, block_e, nvalid, w_gu, b_gu, w_dn, b_dn, layer, chunk=512):
    n_exp, de, d = w_dn.shape[1:]
    assert d == SUBLANES * LANES
    nblocks = rows.shape[0]
    cap = nblocks * MOE_BLOCK
    tile_rows = MOE_BLOCK * SUBLANES
    grid_spec = pltpu.PrefetchScalarGridSpec(
        num_scalar_prefetch=2,
        grid=(nblocks,),
        in_specs=[
            pl.BlockSpec(memory_space=pl.ANY),
            pl.BlockSpec(memory_space=pl.ANY),
            pl.BlockSpec((None, None, d, 2 * de), lambda i, be, nv: (layer, be[i], 0, 0)),
            pl.BlockSpec((None, None, 1, 2 * de), lambda i, be, nv: (layer, be[i], 0, 0)),
            pl.BlockSpec((None, None, de, d), lambda i, be, nv: (layer, be[i], 0, 0)),
            pl.BlockSpec((None, None, 1, d), lambda i, be, nv: (layer, be[i], 0, 0)),
        ],
        out_specs=pl.BlockSpec((tile_rows, LANES), lambda i, be, nv: (i, 0)),
        scratch_shapes=[pltpu.VMEM((2, tile_rows, LANES), F32),
                        pltpu.SMEM((2 * MOE_BLOCK,), jnp.int32),
                        pltpu.SemaphoreType.DMA((2,)),
                        pltpu.SemaphoreType.DMA((2,)),
                        pltpu.VMEM((MOE_BLOCK, de), BF16),
                        pltpu.VMEM((d, 2 * de), BF16),
                        pltpu.VMEM((de, d), BF16)],
    )
    depth = w_dn.shape[0]
    return pl.pallas_call(
        functools.partial(_moe_kernel, chunk=chunk),
        out_shape=jax.ShapeDtypeStruct((cap * SUBLANES, LANES), F32),
        grid_spec=grid_spec,
        compiler_params=pltpu.CompilerParams(dimension_semantics=("arbitrary",),
                                             vmem_limit_bytes=MOE_VMEM_LIMIT),
        name="moe_experts",
    )(block_e, nvalid, rows, x1, w_gu, b_gu.reshape(depth, n_exp, 1, 2 * de),
      w_dn, b_dn.reshape(depth, n_exp, 1, d))


def _combine_kernel(pos_hbm, y_hbm, x_ref, gate_ref, g_ref, b_ref, o_ref,
                    buf_ref, pos_smem, pos_sem, row_sem, *, tm, alpha):
    slot = _pipelined_gather(pos_hbm, y_hbm, buf_ref, pos_smem, pos_sem, row_sem, TOP_K * tm,
                             lambda step: TOP_K * tm)
    gates = gate_ref[...]
    z = alpha * _from_tiles(x_ref, tm)
    for kk in range(TOP_K):
        z = z + gates[:, kk:kk + 1] * _from_tiles(buf_ref.at[slot], tm, base=kk * tm * SUBLANES)
    o_ref[...] = _layer_norm(z, g_ref[...], b_ref[...])


def _combine_norm(y, pos, x1, gates, ln_g, ln_b, alpha, tm=512):
    d = SUBLANES * LANES
    n = x1.shape[0] // SUBLANES
    steps = n // tm
    pos_t = pos.reshape(steps, tm, TOP_K).transpose(0, 2, 1).reshape(steps, TOP_K * tm)
    row = lambda i: (i, 0)
    fixed = lambda i: (0, 0)
    return pl.pallas_call(
        functools.partial(_combine_kernel, tm=tm, alpha=alpha),
        out_shape=jax.ShapeDtypeStruct((n, d), F32),
        grid=(steps,),
        in_specs=[
            pl.BlockSpec(memory_space=pl.ANY),
            pl.BlockSpec(memory_space=pl.ANY),
            pl.BlockSpec((tm * SUBLANES, LANES), row),
            pl.BlockSpec((tm, 8), row),
            pl.BlockSpec((1, d), fixed),
            pl.BlockSpec((1, d), fixed),
        ],
        out_specs=pl.BlockSpec((tm, d), row),
        scratch_shapes=[pltpu.VMEM((2, TOP_K * tm * SUBLANES, LANES), F32),
                        pltpu.SMEM((2 * TOP_K * tm,), jnp.int32),
                        pltpu.SemaphoreType.DMA((2,)),
                        pltpu.SemaphoreType.DMA((2,))],
        compiler_params=_cparams("arbitrary"),
        name="combine_norm",
    )(pos_t, y, x1, gates, ln_g.reshape(1, d), ln_b.reshape(1, d))


def _routing_tables(top_idx, hist, n):
    nk = n * TOP_K
    flat_e = top_idx[:, :TOP_K].reshape(nk)
    assign = jnp.arange(nk, dtype=jnp.int32)
    skey = lax.sort(flat_e * nk + assign)
    order = skey % nk
    counts = jnp.sum(hist[::SUBLANES, :N_EXPERTS], axis=0)
    start = jnp.cumsum(counts) - counts
    _, sorted_at = lax.sort((order, assign), num_keys=1)
    padded = (counts + MOE_BLOCK - 1) // MOE_BLOCK * MOE_BLOCK
    pad_end = jnp.cumsum(padded)
    pad_start = pad_end - padded
    pos = (pad_start[flat_e] + sorted_at - start[flat_e]).astype(jnp.int32)
    n_blocks = nk // MOE_BLOCK + N_EXPERTS + 1
    blk_first = jnp.arange(n_blocks, dtype=jnp.int32) * MOE_BLOCK
    block_e = jnp.sum((pad_end[None, :] <= blk_first[:, None]).astype(jnp.int32), axis=1)
    block_e = jnp.minimum(block_e, N_EXPERTS - 1)
    nvalid = jnp.clip(counts[block_e] - (blk_first - pad_start[block_e]), 0, MOE_BLOCK)
    nvalid = jnp.where(blk_first < pad_end[-1], nvalid, 0).astype(jnp.int32)
    slot = jnp.arange(n_blocks * MOE_BLOCK, dtype=jnp.int32)
    slot_e = jnp.repeat(block_e, MOE_BLOCK)
    within = slot - pad_start[slot_e]
    live = (within < counts[slot_e]) & (slot < pad_end[-1])
    src = order[jnp.clip(start[slot_e] + within, 0, nk - 1)] // TOP_K
    rows = jnp.where(live, src, 0).astype(jnp.int32)
    return rows.reshape(n_blocks, MOE_BLOCK), pos.reshape(n, TOP_K), block_e, nvalid


_EVEN_PLAN = (
    (512, "attn", HEAD_DIM ** -0.5), (128, "attn", 1.0), (128, None, 1.0),
    (512, None, 1.0), (512, None, 1.0), (512, None, 1.0), (512, None, 1.0),
)
_ODD_PLAN = (
    (256, "ret", 1.0), (256, "ret", C_KEY_DIM ** -0.5), (512, None, 1.0), (512, None, 1.0),
) + ((512, "attn", HEAD_DIM ** -0.5), (128, "attn", 1.0), (128, None, 1.0)) * len(D_PATTERNS)


def kernel(x, w_in_even, b_in_even, attn_sinks, hgrn_lb_logits, hgrn_norm, w_out_even, w_in_odd,
           b_in_odd, ret_norm, w_out_odd, ln1_g, ln1_b, ln2_g, ln2_b, router_w, router_b,
           expert_w_gu, expert_b_gu, expert_w_dn, expert_b_dn):
    batch, seq, d = x.shape
    n = batch * seq
    depth = ln1_g.shape[0]
    alpha = float((2 * depth) ** 0.25)

    rope_inv = 1.0 / (ROPE_THETA ** (jnp.arange(0, ROPE_DIM, 2, dtype=F32) / ROPE_DIM))
    ret_inv = 1.0 / (RET_THETA ** jnp.linspace(0.0, 1.0, C_KEY_DIM // 2, dtype=F32))
    tab_attn = _rope_tables(seq, ROPE_DIM // 2, rope_inv, HEAD_DIM)
    tab_ret = _rope_tables(seq, C_KEY_DIM // 2, ret_inv, C_KEY_DIM)
    lb_soft = jax.nn.softmax(hgrn_lb_logits.astype(F32), axis=0)
    lower_bounds = jnp.concatenate(
        [jnp.zeros_like(lb_soft[:1]), jnp.cumsum(lb_soft, axis=0)[:-1]], axis=0)

    xf = x.reshape(n, d)
    for layer in range(depth):
        j = layer // 2
        if layer % 2 == 0:
            aq, ak, av, bq, bf, bi, bg = _inproj(
                xf, w_in_even[j].astype(BF16), b_in_even[j].reshape(1, -1), tab_attn, tab_ret,
                _EVEN_PLAN, seq)
            (ya,) = _banded_attention(aq, ak, av, batch, seq, 1, A_WINDOW - 1,
                                      sinks=attn_sinks[j])
            yb = _hgrn2(bq, bf, bi, bg, lower_bounds[j], hgrn_norm[j], batch, seq)
            w_out = w_out_even[j]
        else:
            outs = _inproj(xf, w_in_odd[j].astype(BF16), b_in_odd[j].reshape(1, -1), tab_attn,
                           tab_ret, _ODD_PLAN, seq)
            cq, ck, cv, cg = outs[:4]
            ya = _retention(cq, ck, cv, cg, ret_norm[j], batch, seq)
            parts = []
            for p, (window, dil) in enumerate(D_PATTERNS):
                dq, dk, dv = outs[4 + 3 * p:7 + 3 * p]
                o, lse = _banded_attention(dq, dk, dv, batch, seq, dil, window // dil,
                                           want_lse=True, out_dtype=F32)
                parts += [o, lse]
            yb = _merge_patterns(parts)
            w_out = w_out_odd[j]
        x1, top_idx, gates, hist = _outproj_norm_route(
            ya, yb, w_out, xf, ln1_g[layer], ln1_b[layer], router_w[layer], router_b[layer], alpha)
        rows, pos, block_e, nvalid = _routing_tables(top_idx, hist, n)
        y = _moe_experts(x1, rows, block_e, nvalid, expert_w_gu, expert_b_gu, expert_w_dn,
                         expert_b_dn, layer)
        xf = _combine_norm(y, pos, x1, gates, ln2_g[layer], ln2_b[layer], alpha)
    return xf.reshape(batch, seq, d)
```

```python
import functools
import math

import jax
import jax.numpy as jnp
from jax import lax
from jax.experimental import pallas as pl
from jax.experimental.pallas import tpu as pltpu

F32 = jnp.float32
BF16 = jnp.bfloat16

HEAD_DIM = 64
ATTN_BLOCK = 128
A_Q_HEADS = 8
A_KV_HEADS = 2
A_WINDOW = 128
B_HEADS = 4
B_KEY_DIM = 128
B_MIN_F = 1e-30
C_HEADS = 4
C_KEY_DIM = 64
C_VAL_DIM = 128
D_PATTERNS = ((128, 1), (512, 4), (2048, 16))
ROPE_THETA = 500000.0
ROPE_DIM = HEAD_DIM // 4
RET_THETA = 10000.0
N_EXPERTS = 32
TOP_K = 4
SWIGLU_LIMIT = 7.0
SWIGLU_ALPHA = 1.702
MOE_BLOCK = 512
LN_EPS = 1e-5
NORM_EPS = 1e-6
NEG_BIG = -1e30

LANES = 128
SUBLANES = 8
VMEM_LIMIT = 48 * 1024 * 1024
MOE_VMEM_LIMIT = 56 * 1024 * 1024

HGRN_CHUNK = 128
HGRN_SUB = 8
HGRN_LEVELS = (16, 32, 64, 128)


def _cparams(*sem):
    return pltpu.CompilerParams(dimension_semantics=sem, vmem_limit_bytes=VMEM_LIMIT)


def _to_tiles(ref, value):
    rows = value.shape[0]
    for c in range(SUBLANES):
        ref[pl.ds(c, rows, stride=SUBLANES), :] = value[:, c * LANES:(c + 1) * LANES]


def _from_tiles(ref, rows, base=0):
    return jnp.concatenate(
        [ref[pl.ds(base + c, rows, stride=SUBLANES), :] for c in range(SUBLANES)], axis=1)


def _sigmoid(z):
    return 1.0 / (1.0 + jnp.exp(-z))


def _rotate(h, cos, sa, sb, shift):
    width = h.shape[1]
    rep = width // LANES
    if rep > 1:
        cos = jnp.concatenate([cos] * rep, axis=1)
        sa = jnp.concatenate([sa] * rep, axis=1)
        sb = jnp.concatenate([sb] * rep, axis=1)
    up = pltpu.roll(h, width - shift, axis=1)
    dn = pltpu.roll(h, shift, axis=1)
    return h * cos + up * sa + dn * sb


def _inproj_kernel(x_ref, w_ref, b_ref, ta_ref, tr_ref, *out_refs, plan):
    xb = x_ref[...].astype(BF16)
    off = 0
    for o_ref, (width, rope, scale) in zip(out_refs, plan):
        h = jnp.dot(xb, w_ref[:, off:off + width], preferred_element_type=F32)
        h = h + b_ref[:, off:off + width]
        if rope == "attn":
            h = _rotate(h, ta_ref[0], ta_ref[1], ta_ref[2], ROPE_DIM // 2)
        elif rope == "ret":
            h = _rotate(h, tr_ref[0], tr_ref[1], tr_ref[2], C_KEY_DIM // 2)
        if scale != 1.0:
            h = h * scale
        o_ref[...] = h.astype(o_ref.dtype)
        off += width


def _inproj(x, w, b, tab_attn, tab_ret, plan, seq, tm=512):
    n, d = x.shape
    cols = w.shape[1]
    tpb = seq // tm
    kern = functools.partial(_inproj_kernel, plan=plan)
    outs = tuple(jax.ShapeDtypeStruct((n, wd), F32) for wd, _, _ in plan)
    return pl.pallas_call(
        kern,
        out_shape=outs,
        grid=(n // tm,),
        in_specs=[
            pl.BlockSpec((tm, d), lambda i: (i, 0)),
            pl.BlockSpec((d, cols), lambda i: (0, 0)),
            pl.BlockSpec((1, cols), lambda i: (0, 0)),
            pl.BlockSpec((3, tm, LANES), lambda i: (0, i % tpb, 0)),
            pl.BlockSpec((3, tm, LANES), lambda i: (0, i % tpb, 0)),
        ],
        out_specs=tuple(pl.BlockSpec((tm, wd), lambda i: (i, 0)) for wd, _, _ in plan),
        compiler_params=_cparams("parallel"),
        name="inproj",
    )(x, w, b, tab_attn, tab_ret)


def _rope_tables(seq, half, inv_freq, period):
    pos = jnp.arange(seq, dtype=F32)
    ang = pos[:, None] * inv_freq[None, :]
    cos, sin = jnp.cos(ang), jnp.sin(ang)
    pad = period - 2 * half
    ones = jnp.ones((seq, pad), F32)
    zeros = jnp.zeros((seq, pad), F32)
    zh = jnp.zeros((seq, half), F32)
    cosf = jnp.concatenate([cos, cos, ones], axis=1)
    sa = jnp.concatenate([-sin, zh, zeros], axis=1)
    sb = jnp.concatenate([zh, sin, zeros], axis=1)
    rep = LANES // period
    return jnp.stack([jnp.tile(t, (1, rep)) for t in (cosf, sa, sb)], axis=0)


def _attn_kernel(*refs, dil, max_dist, nsub, has_sink, want_lse):
    if has_sink:
        sink_ref, refs = refs[0], refs[1:]
    q_refs = refs[:4]
    k_ref, v_ref, kp_ref, vp_ref = refs[4:8]
    o_ref = refs[8]
    lse_ref = refs[9] if want_lse else None
    n_out = 2 if want_lse else 1
    stage = refs[8 + n_out:]
    blk = ATTN_BLOCK
    qi = lax.broadcasted_iota(jnp.int32, (4 * blk, 2 * blk), 0) % blk
    kj = lax.broadcasted_iota(jnp.int32, (4 * blk, 2 * blk), 1)
    dist = qi - kj + blk
    band = (dist >= 0) & (dist <= max_dist)
    first_kmin = jnp.where(pl.program_id(1) == 0, blk, 0)
    band_first = band & (kj >= first_kmin)
    row4 = lax.broadcasted_iota(jnp.int32, (4 * blk, 1), 0) // blk

    def take(start):
        return pl.ds(start, blk, stride=dil) if dil > 1 else pl.ds(start, blk)

    def one_group(r, j):
        rows = take(j * blk * dil + r)
        if j == 0:
            k_prev, v_prev = kp_ref[take(r), :], vp_ref[take(r), :]
            valid = band_first
        else:
            prow = take((j - 1) * blk * dil + r)
            k_prev, v_prev = k_ref[prow, :], v_ref[prow, :]
            valid = band
        k_cat = jnp.concatenate([k_prev, k_ref[rows, :]], axis=0).astype(BF16)
        v_cat = jnp.concatenate([v_prev, v_ref[rows, :]], axis=0).astype(BF16)
        for h in range(A_KV_HEADS):
            qa, qb = q_refs[2 * h][rows, :], q_refs[2 * h + 1][rows, :]
            q4 = jnp.concatenate([qa[:, :HEAD_DIM], qa[:, HEAD_DIM:], qb[:, :HEAD_DIM],
                                  qb[:, HEAD_DIM:]], axis=0).astype(BF16)
            kh = k_cat[:, h * HEAD_DIM:(h + 1) * HEAD_DIM]
            vh = v_cat[:, h * HEAD_DIM:(h + 1) * HEAD_DIM]
            s = lax.dot_general(q4, kh, (((1,), (1,)), ((), ())), preferred_element_type=F32)
            s = jnp.where(valid, s, NEG_BIG)
            m = jnp.max(s, axis=1, keepdims=True)
            if has_sink:
                sk = jnp.full((4 * blk, 1), sink_ref[4 * h + 3], F32)
                for g in range(3):
                    sk = jnp.where(row4 == g, sink_ref[4 * h + g], sk)
                m = jnp.maximum(m, sk)
            p = jnp.exp(s - m)
            den = jnp.sum(p, axis=1, keepdims=True)
            if has_sink:
                den = den + jnp.exp(sk - m)
            o = jnp.dot(p.astype(BF16), vh, preferred_element_type=F32) / den
            lse = m + jnp.log(den) if want_lse else None
            for half in range(2):
                cb = 2 * h + half
                g0 = 2 * half
                pair = jnp.concatenate(
                    [o[g0 * blk:(g0 + 1) * blk], o[(g0 + 1) * blk:(g0 + 2) * blk]], axis=1)
                if dil > 1:
                    stage[0][cb, rows, :] = pair
                else:
                    o_ref[rows, cb * LANES:(cb + 1) * LANES] = pair.astype(o_ref.dtype)
                if want_lse:
                    lpair = jnp.concatenate(
                        [jnp.broadcast_to(lse[g * blk:(g + 1) * blk], (blk, HEAD_DIM))
                         for g in (g0, g0 + 1)], axis=1)
                    if dil > 1:
                        stage[1][cb, rows, :] = lpair
                    else:
                        lse_ref[rows, cb * LANES:(cb + 1) * LANES] = lpair

    for j in range(nsub):
        if dil == 1:
            one_group(0, j)
        else:
            def body(r, carry, j=j):
                one_group(r, j)
                return carry
            lax.fori_loop(0, dil, body, 0)
    if dil > 1:
        for cb in range(4):
            o_ref[:, cb * LANES:(cb + 1) * LANES] = stage[0][cb].astype(o_ref.dtype)
            if want_lse:
                lse_ref[:, cb * LANES:(cb + 1) * LANES] = stage[1][cb]


def _banded_attention(q, k, v, batch, seq, dil, max_dist, sinks=None, want_lse=False,
                      out_dtype=BF16):
    span = ATTN_BLOCK * dil
    nsub = max(1, 512 // span)
    tb = span * nsub
    nt = seq // tb
    qw, kw = q.shape[1], k.shape[1]
    q3, k3, v3 = (t.reshape(batch, seq, t.shape[1]) for t in (q, k, v))
    kern = functools.partial(_attn_kernel, dil=dil, max_dist=max_dist, nsub=nsub,
                             has_sink=sinks is not None, want_lse=want_lse)
    cur = lambda b, t: (b, t, 0)
    prev = lambda b, t: (b, jnp.maximum(t * nsub - 1, 0), 0)
    in_specs = [pl.BlockSpec((None, tb, LANES), lambda b, t, c=c: (b, t, c)) for c in range(4)] + [
        pl.BlockSpec((None, tb, kw), cur),
        pl.BlockSpec((None, tb, kw), cur),
        pl.BlockSpec((None, span, kw), prev),
        pl.BlockSpec((None, span, kw), prev),
    ]
    args = [q3, q3, q3, q3, k3, v3, k3, v3]
    if sinks is not None:
        in_specs = [pl.BlockSpec(memory_space=pltpu.SMEM)] + in_specs
        args = [sinks.astype(F32)] + args
    n_out = 2 if want_lse else 1
    out_shape = [jax.ShapeDtypeStruct((batch, seq, qw), out_dtype)]
    out_specs = [pl.BlockSpec((None, tb, qw), cur)]
    if want_lse:
        out_shape.append(jax.ShapeDtypeStruct((batch, seq, qw), F32))
        out_specs.append(pl.BlockSpec((None, tb, qw), cur))
    scratch = [pltpu.VMEM((4, tb, LANES), F32)] * n_out if dil > 1 else []
    res = pl.pallas_call(
        kern,
        out_shape=tuple(out_shape),
        grid=(batch, nt),
        in_specs=in_specs,
        out_specs=tuple(out_specs),
        scratch_shapes=scratch,
        compiler_params=_cparams("parallel", "arbitrary"),
        name=f"banded_attn_d{dil}",
    )(*args)
    return tuple(r.reshape(batch * seq, qw) for r in res)


def _merge_kernel(o1, l1, o2, l2, o3, l3, out_ref):
    a, b, c = l1[...], l2[...], l3[...]
    m = jnp.maximum(jnp.maximum(a, b), c)
    ea, eb, ec = jnp.exp(a - m), jnp.exp(b - m), jnp.exp(c - m)
    num = ea * o1[...] + eb * o2[...] + ec * o3[...]
    out_ref[...] = (num / (ea + eb + ec)).astype(out_ref.dtype)


def _merge_patterns(parts, tm=512):
    n, w = parts[0].shape
    spec = pl.BlockSpec((tm, w), lambda i: (i, 0))
    return pl.pallas_call(
        _merge_kernel,
        out_shape=jax.ShapeDtypeStruct((n, w), BF16),
        grid=(n // tm,),
        in_specs=[spec] * 6,
        out_specs=spec,
        compiler_params=_cparams("parallel"),
        name="merge_patterns",
    )(*parts)


def _hgrn_structure():
    c = HGRN_CHUNK
    r = lax.broadcasted_iota(jnp.int32, (c, c), 0)
    j = lax.broadcasted_iota(jnp.int32, (c, c), 1)
    blocks = [(j <= r)]
    for lv in HGRN_LEVELS:
        mid = (r // lv) * lv + lv // 2 - 1
        second = (r % lv) >= lv // 2
        lo = jnp.where(second, mid, r)
        hi = jnp.where(second, r, mid)
        blocks.append((j > lo) & (j <= hi))
    return jnp.concatenate([jnp.where(m, 1.0, 0.0) for m in blocks], axis=0).astype(BF16)


def _split3(x):
    hi = x.astype(BF16)
    r1 = x - hi.astype(F32)
    mid = r1.astype(BF16)
    lo = (r1 - mid.astype(F32)).astype(BF16)
    return hi, mid, lo


def _hgrn_kernel(q_ref, f_ref, i_ref, g_ref, lb_ref, gain_ref, o_ref, st_ref, *, nchunk):
    c = HGRN_CHUNK

    @pl.when(pl.program_id(2) == 0)
    def _():
        st_ref[...] = jnp.zeros_like(st_ref)

    dmat = _hgrn_structure()
    row = lax.broadcasted_iota(jnp.int32, (c, c), 0)
    col = lax.broadcasted_iota(jnp.int32, (c, c), 1)
    lb = jnp.clip(lb_ref[...], 0.0, 1.0)
    gain = gain_ref[...]
    sub_row = lax.broadcasted_iota(jnp.int32, (HGRN_SUB, c), 0)
    sub_col = lax.broadcasted_iota(jnp.int32, (HGRN_SUB, c), 1)

    for ci in range(nchunk):
        rows = pl.ds(ci * c, c)
        q = q_ref[rows, :]
        f = lb + (1.0 - lb) * _sigmoid(f_ref[rows, :])
        lf = jnp.log(jnp.maximum(f, B_MIN_F))
        key = 1.0 - f
        val = i_ref[rows, :].astype(BF16)

        hi, mid, lo = _split3(lf)
        sums = (jnp.dot(dmat, hi, preferred_element_type=F32)
                + jnp.dot(dmat, mid, preferred_element_type=F32)
                + jnp.dot(dmat, lo, preferred_element_type=F32))
        b = sums[0:c]

        att = jnp.zeros((c, c), F32)
        for n, lv in enumerate(HGRN_LEVELS):
            e = jnp.exp(sums[(n + 1) * c:(n + 2) * c])
            second = (row % lv) >= lv // 2
            qs = jnp.where(second, q * e, 0.0).astype(BF16)
            ks = jnp.where(second, 0.0, key * e).astype(BF16)
            a = lax.dot_general(qs, ks, (((1,), (1,)), ((), ())), preferred_element_type=F32)
            att = att + jnp.where((row // lv) == (col // lv), a, 0.0)

        diag = []
        for a0 in range(0, c, HGRN_SUB):
            qa, ka, ba = q[a0:a0 + HGRN_SUB], key[a0:a0 + HGRN_SUB], b[a0:a0 + HGRN_SUB]
            acc = jnp.zeros((HGRN_SUB, c), F32)
            for s in range(HGRN_SUB):
                e = jnp.exp(jnp.minimum(ba - ba[s:s + 1, :], 0.0))
                w = jnp.sum(qa * e * ka[s:s + 1, :], axis=1, keepdims=True)
                acc = jnp.where((sub_col == a0 + s) & (sub_row >= s), w, acc)
            diag.append(acc)
        att = att + jnp.concatenate(diag, axis=0)

        st = st_ref[...]
        b_last = b[c - 1:c, :]
        q_in = (q * jnp.exp(b)).astype(BF16)
        k_out = (key * jnp.exp(b_last - b)).astype(BF16)
        o = jnp.dot(att.astype(BF16), val, preferred_element_type=F32)
        o = o + lax.dot_general(q_in, st.astype(BF16), (((1,), (1,)), ((), ())),
                                preferred_element_type=F32)
        st_ref[...] = jnp.exp(b_last) * st + lax.dot_general(
            val, k_out, (((0,), (0,)), ((), ())), preferred_element_type=F32)

        o = o * lax.rsqrt(jnp.mean(o * o, axis=1, keepdims=True) + NORM_EPS)
        g = g_ref[rows, :]
        o_ref[rows, :] = (o * gain * (g * _sigmoid(g))).astype(o_ref.dtype)


def _hgrn2(q, f, i_in, g, lb, gain, batch, seq, nchunk=4):
    width = q.shape[1]
    heads = width // B_KEY_DIM
    tb = HGRN_CHUNK * nchunk
    q3, f3, i3, g3 = (t.reshape(batch, seq, width) for t in (q, f, i_in, g))
    blk = pl.BlockSpec((None, tb, B_KEY_DIM), lambda b, h, t: (b, t, h))
    vec = pl.BlockSpec((1, B_KEY_DIM), lambda b, h, t: (0, h))
    out = pl.pallas_call(
        functools.partial(_hgrn_kernel, nchunk=nchunk),
        out_shape=jax.ShapeDtypeStruct((batch, seq, width), BF16),
        grid=(batch, heads, seq // tb),
        in_specs=[blk, blk, blk, blk, vec, vec],
        out_specs=blk,
        scratch_shapes=[pltpu.VMEM((B_KEY_DIM, B_KEY_DIM), F32)],
        compiler_params=_cparams("parallel", "parallel", "arbitrary"),
        name="hgrn2",
    )(q3, f3, i3, g3, lb.reshape(1, width).astype(F32), gain.reshape(1, width).astype(F32))
    return out.reshape(batch * seq, width)


def _retention_kernel(q_ref, k_ref, v_ref, g_ref, gain_ref, o_ref, st_ref, *, nchunk):
    c = ATTN_BLOCK

    @pl.when(pl.program_id(1) == 0)
    def _():
        st_ref[...] = jnp.zeros_like(st_ref)

    ri = lax.broadcasted_iota(jnp.int32, (c, c), 0)
    ci_ = lax.broadcasted_iota(jnp.int32, (c, c), 1)
    rel = (ri - ci_).astype(F32)
    rowf = ri.astype(F32)

    for h in range(C_HEADS):
        log_gamma = math.log1p(-(2.0 ** (-5.0 - h)))
        decay = jnp.where(rel >= 0.0, jnp.exp(log_gamma * jnp.maximum(rel, 0.0)), 0.0)
        head_scale = jnp.exp(log_gamma * (rowf + 1.0))
        tail_scale = jnp.exp(log_gamma * (c - 1.0 - rowf))
        chunk_decay = math.exp(log_gamma * c)
        gain = gain_ref[:, h * C_VAL_DIM:(h + 1) * C_VAL_DIM]
        for ci in range(nchunk):
            rows = pl.ds(ci * c, c)
            q = q_ref[rows, h * C_KEY_DIM:(h + 1) * C_KEY_DIM].astype(BF16)
            k = k_ref[rows, h * C_KEY_DIM:(h + 1) * C_KEY_DIM].astype(BF16)
            v = v_ref[rows, h * C_VAL_DIM:(h + 1) * C_VAL_DIM]
            st = st_ref[h]
            s = lax.dot_general(q, k, (((1,), (1,)), ((), ())), preferred_element_type=F32)
            o = jnp.dot((s * decay).astype(BF16), v.astype(BF16), preferred_element_type=F32)
            o = o + head_scale * jnp.dot(q, st.astype(BF16), preferred_element_type=F32)
            st_ref[h] = chunk_decay * st + lax.dot_general(
                k, (v * tail_scale).astype(BF16), (((0,), (0,)), ((), ())),
                preferred_element_type=F32)
            o = o - jnp.mean(o, axis=1, keepdims=True)
            o = o * lax.rsqrt(jnp.mean(o * o, axis=1, keepdims=True) + NORM_EPS)
            g = g_ref[rows, h * C_VAL_DIM:(h + 1) * C_VAL_DIM]
            o_ref[rows, h * C_VAL_DIM:(h + 1) * C_VAL_DIM] = (
                o * gain * (g * _sigmoid(g))).astype(o_ref.dtype)


def _retention(q, k, v, g, gain, batch, seq, nchunk=4):
    tb = ATTN_BLOCK * nchunk
    kw, vw = q.shape[1], v.shape[1]
    q3, k3 = q.reshape(batch, seq, kw), k.reshape(batch, seq, kw)
    v3, g3 = v.reshape(batch, seq, vw), g.reshape(batch, seq, vw)
    kspec = pl.BlockSpec((None, tb, kw), lambda b, t: (b, t, 0))
    vspec = pl.BlockSpec((None, tb, vw), lambda b, t: (b, t, 0))
    out = pl.pallas_call(
        functools.partial(_retention_kernel, nchunk=nchunk),
        out_shape=jax.ShapeDtypeStruct((batch, seq, vw), BF16),
        grid=(batch, seq // tb),
        in_specs=[kspec, kspec, vspec, vspec, pl.BlockSpec((1, vw), lambda b, t: (0, 0))],
        out_specs=vspec,
        scratch_shapes=[pltpu.VMEM((C_HEADS, C_KEY_DIM, C_VAL_DIM), F32)],
        compiler_params=_cparams("parallel", "arbitrary"),
        name="retention",
    )(q3, k3, v3, g3, gain.reshape(1, vw).astype(F32))
    return out.reshape(batch * seq, vw)


def _layer_norm(z, g, b):
    mu = jnp.mean(z, axis=1, keepdims=True)
    zc = z - mu
    var = jnp.mean(zc * zc, axis=1, keepdims=True)
    return zc * lax.rsqrt(var + LN_EPS) * g + b


def _outproj_kernel(ya_ref, yb_ref, w_ref, x_ref, g_ref, b_ref, rwh_ref, rwl_ref, rb_ref,
                    x1_ref, idx_ref, gate_ref, rank_ref, hist_ref, run_ref, *, alpha):
    @pl.when(pl.program_id(0) == 0)
    def _():
        run_ref[...] = jnp.zeros_like(run_ref)

    half = ya_ref.shape[1]
    mix = jnp.dot(ya_ref[...], w_ref[0:half, :], preferred_element_type=F32)
    mix = mix + jnp.dot(yb_ref[...], w_ref[half:, :], preferred_element_type=F32)
    x1 = _layer_norm(alpha * x_ref[...] + mix, g_ref[...], b_ref[...])
    _to_tiles(x1_ref, x1)

    xh = x1.astype(BF16)
    xl = (x1 - xh.astype(F32)).astype(BF16)
    logits = (jnp.dot(xh, rwh_ref[...], preferred_element_type=F32)
              + jnp.dot(xh, rwl_ref[...], preferred_element_type=F32)
              + jnp.dot(xl, rwh_ref[...], preferred_element_type=F32)) + rb_ref[...]
    lane = lax.broadcasted_iota(jnp.int32, logits.shape, 1)
    idx_acc = jnp.zeros(logits.shape, jnp.int32)
    val_acc = jnp.zeros(logits.shape, F32)
    top0 = None
    den = None
    masks = []
    for kk in range(TOP_K):
        m = jnp.max(logits, axis=1, keepdims=True)
        sel = jnp.min(jnp.where(logits == m, lane, LANES), axis=1, keepdims=True)
        if kk == 0:
            top0 = m
        e = jnp.exp(m - top0)
        den = e if den is None else den + e
        idx_acc = jnp.where(lane == kk, sel, idx_acc)
        val_acc = jnp.where(lane == kk, e, val_acc)
        picked = lane == sel
        masks.append(picked)
        onehot = jnp.where(picked, 1.0, 0.0)
        chosen = onehot if kk == 0 else chosen + onehot
        logits = jnp.where(picked, NEG_BIG * 2.0, logits)
    idx_ref[...] = idx_acc[:, 0:idx_ref.shape[1]]
    gate_ref[...] = (val_acc / den)[:, 0:gate_ref.shape[1]]

    tm = logits.shape[0]
    ri = lax.broadcasted_iota(jnp.int32, (tm, tm), 0)
    ci = lax.broadcasted_iota(jnp.int32, (tm, tm), 1)
    earlier = jnp.where(ci < ri, 1.0, 0.0).astype(BF16)
    before = jnp.dot(earlier, chosen.astype(BF16), preferred_element_type=F32) + run_ref[...]
    rank_acc = jnp.zeros(logits.shape, F32)
    for kk in range(TOP_K):
        rk = jnp.sum(jnp.where(masks[kk], before, 0.0), axis=1, keepdims=True)
        rank_acc = jnp.where(lane == kk, rk, rank_acc)
    rank_ref[...] = rank_acc[:, 0:rank_ref.shape[1]].astype(jnp.int32)
    run_ref[...] = run_ref[...] + jnp.sum(chosen, axis=0, keepdims=True)
    hist_ref[...] = jnp.broadcast_to(run_ref[...], hist_ref.shape).astype(jnp.int32)


def _outproj_norm_route(ya, yb, w_out, x, ln_g, ln_b, rw, rb, alpha, tm=512):
    n, d = x.shape
    half = ya.shape[1]
    rw_pad = jnp.zeros((d, LANES), F32).at[:, :N_EXPERTS].set(rw.astype(F32))
    rwh = rw_pad.astype(BF16)
    rwl = (rw_pad - rwh.astype(F32)).astype(BF16)
    rb_pad = jnp.full((1, LANES), NEG_BIG, F32).at[0, :N_EXPERTS].set(rb.astype(F32))
    row = lambda i: (i, 0)
    fixed = lambda i: (0, 0)
    return pl.pallas_call(
        functools.partial(_outproj_kernel, alpha=alpha),
        out_shape=(jax.ShapeDtypeStruct((n * SUBLANES, LANES), F32),
                   jax.ShapeDtypeStruct((n, 8), jnp.int32),
                   jax.ShapeDtypeStruct((n, 8), F32),
                   jax.ShapeDtypeStruct((n, 8), jnp.int32),
                   jax.ShapeDtypeStruct((n // tm * SUBLANES, LANES), jnp.int32)),
        grid=(n // tm,),
        in_specs=[
            pl.BlockSpec((tm, half), row),
            pl.BlockSpec((tm, half), row),
            pl.BlockSpec((2 * half, d), fixed),
            pl.BlockSpec((tm, d), row),
            pl.BlockSpec((1, d), fixed),
            pl.BlockSpec((1, d), fixed),
            pl.BlockSpec((d, LANES), fixed),
            pl.BlockSpec((d, LANES), fixed),
            pl.BlockSpec((1, LANES), fixed),
        ],
        out_specs=(pl.BlockSpec((tm * SUBLANES, LANES), row),
                   pl.BlockSpec((tm, 8), row),
                   pl.BlockSpec((tm, 8), row),
                   pl.BlockSpec((tm, 8), row),
                   pl.BlockSpec((SUBLANES, LANES), row)),
        scratch_shapes=[pltpu.VMEM((1, LANES), F32)],
        compiler_params=_cparams("arbitrary"),
        name="outproj_norm_route",
    )(ya, yb, w_out.astype(BF16), x, ln_g.reshape(1, d), ln_b.reshape(1, d), rwh, rwl, rb_pad)


ISSUE_UNROLL = 8
DRAIN_UNROLL = 16


def _issue_tiles(idx_smem, base, src_hbm, dst_ref, sem, count):
    def body(g, carry):
        for u in range(ISSUE_UNROLL):
            r = g * ISSUE_UNROLL + u
            src = pl.multiple_of(idx_smem[base + r] * SUBLANES, SUBLANES)
            dst = pl.multiple_of(r * SUBLANES, SUBLANES)
            pltpu.make_async_copy(src_hbm.at[pl.ds(src, SUBLANES), :],
                                  dst_ref.at[pl.ds(dst, SUBLANES), :], sem).start(priority=u % 2)
        return carry

    lax.fori_loop(0, count // ISSUE_UNROLL, body, 0)


def _drain_tiles(src_hbm, dst_ref, sem, count):
    def body(r, carry):
        dst = pl.multiple_of(r * SUBLANES, SUBLANES)
        pltpu.make_async_copy(src_hbm.at[pl.ds(0, SUBLANES), :],
                              dst_ref.at[pl.ds(dst, SUBLANES), :], sem).wait()
        return carry

    def group(g, carry):
        for u in range(DRAIN_UNROLL):
            body(g * DRAIN_UNROLL + u, carry)
        return carry

    lax.fori_loop(0, count // DRAIN_UNROLL, group, 0)


def _pipelined_gather(idx_hbm, src_hbm, buf, idx_smem, idx_sem, row_sem, count, rows_of):
    i = pl.program_id(0)
    n = pl.num_programs(0)
    slot = i % 2
    nslot = 1 - slot

    def idx_copy(step, s):
        return pltpu.make_async_copy(idx_hbm.at[step], idx_smem.at[pl.ds(s * count, count)],
                                     idx_sem.at[s])

    def issue(step, s):
        _issue_tiles(idx_smem, s * count, src_hbm, buf.at[s], row_sem.at[s], rows_of(step))

    @pl.when(i == 0)
    def _():
        idx_copy(0, 0).start()
        idx_copy(0, 0).wait()
        issue(0, 0)

        @pl.when(n > 1)
        def _():
            idx_copy(1, 1).start()

    @pl.when(i + 1 < n)
    def _():
        idx_copy(i + 1, nslot).wait()
        issue(i + 1, nslot)

    @pl.when(i + 2 < n)
    def _():
        idx_copy(i + 2, slot).start()

    _drain_tiles(src_hbm, buf.at[slot], row_sem.at[slot], rows_of(i))
    return slot


def _moe_kernel(be_ref, nv_ref, rows_hbm, x_hbm, wgu_ref, bgu_ref, wdn_ref, bdn_ref, y_ref,
                xbuf, idx_smem, idx_sem, row_sem, act_ref, wgu_bf, wdn_bf, *, chunk):
    i = pl.program_id(0)
    de = wdn_ref.shape[0]
    @pl.when(i == 0)
    def _():
        xbuf[...] = jnp.zeros_like(xbuf)

    slot = _pipelined_gather(
        rows_hbm, x_hbm, xbuf, idx_smem, idx_sem, row_sem, MOE_BLOCK,
        lambda step: (nv_ref[step] + DRAIN_UNROLL - 1) // DRAIN_UNROLL * DRAIN_UNROLL)

    @pl.when((i == 0) | (be_ref[i] != be_ref[jnp.maximum(i - 1, 0)]))
    def _():
        for c0 in range(0, 2 * de, chunk):
            wgu_bf[:, c0:c0 + chunk] = wgu_ref[:, c0:c0 + chunk].astype(BF16)
        for c0 in range(0, wdn_ref.shape[1], chunk):
            wdn_bf[:, c0:c0 + chunk] = wdn_ref[:, c0:c0 + chunk].astype(BF16)

    @pl.when(nv_ref[i] > 0)
    def _():
        xb = _from_tiles(xbuf.at[slot], MOE_BLOCK).astype(BF16)
        for c0 in range(0, de, chunk):
            gate = jnp.dot(xb, wgu_bf[:, c0:c0 + chunk], preferred_element_type=F32)
            gate = gate + bgu_ref[:, c0:c0 + chunk]
            up = jnp.dot(xb, wgu_bf[:, de + c0:de + c0 + chunk], preferred_element_type=F32)
            up = up + bgu_ref[:, de + c0:de + c0 + chunk]
            gate = jnp.minimum(gate, SWIGLU_LIMIT)
            up = jnp.clip(up, -SWIGLU_LIMIT, SWIGLU_LIMIT)
            act = (up + 1.0) * gate * _sigmoid(SWIGLU_ALPHA * gate)
            act_ref[:, c0:c0 + chunk] = act.astype(BF16)
        _to_tiles(y_ref, jnp.dot(act_ref[...], wdn_bf[...], preferred_element_type=F32)
                  + bdn_ref[...])

    @pl.when(nv_ref[i] == 0)
    def _():
        y_ref[...] = jnp.zeros_like(y_ref)


def _moe_experts(x1, rows, block_e, nvalid, w_gu, b_gu, w_dn, b_dn, layer, chunk=512):
    n_exp, de, d = w_dn.shape[1:]
    assert d == SUBLANES * LANES
    nblocks = rows.shape[0]
    cap = nblocks * MOE_BLOCK
    tile_rows = MOE_BLOCK * SUBLANES
    grid_spec = pltpu.PrefetchScalarGridSpec(
        num_scalar_prefetch=2,
        grid=(nblocks,),
        in_specs=[
            pl.BlockSpec(memory_space=pl.ANY),
            pl.BlockSpec(memory_space=pl.ANY),
            pl.BlockSpec((None, None, d, 2 * de), lambda i, be, nv: (layer, be[i], 0, 0)),
            pl.BlockSpec((None, None, 1, 2 * de), lambda i, be, nv: (layer, be[i], 0, 0)),
            pl.BlockSpec((None, None, de, d), lambda i, be, nv: (layer, be[i], 0, 0)),
            pl.BlockSpec((None, None, 1, d), lambda i, be, nv: (layer, be[i], 0, 0)),
        ],
        out_specs=pl.BlockSpec((tile_rows, LANES), lambda i, be, nv: (i, 0)),
        scratch_shapes=[pltpu.VMEM((2, tile_rows, LANES), F32),
                        pltpu.SMEM((2 * MOE_BLOCK,), jnp.int32),
                        pltpu.SemaphoreType.DMA((2,)),
                        pltpu.SemaphoreType.DMA((2,)),
                        pltpu.VMEM((MOE_BLOCK, de), BF16),
                        pltpu.VMEM((d, 2 * de), BF16),
                        pltpu.VMEM((de, d), BF16)],
    )
    depth = w_dn.shape[0]
    return pl.pallas_call(
        functools.partial(_moe_kernel, chunk=chunk),
        out_shape=jax.ShapeDtypeStruct((cap * SUBLANES, LANES), F32),
        grid_spec=grid_spec,
        compiler_params=pltpu.CompilerParams(dimension_semantics=("arbitrary",),
                                             vmem_limit_bytes=MOE_VMEM_LIMIT),
        name="moe_experts",
    )(block_e, nvalid, rows, x1, w_gu, b_gu.reshape(depth, n_exp, 1, 2 * de),
      w_dn, b_dn.reshape(depth, n_exp, 1, d))


def _combine_kernel(pos_hbm, y_hbm, x_ref, gate_ref, g_ref, b_ref, o_ref,
                    buf_ref, pos_smem, pos_sem, row_sem, *, tm, alpha):
    slot = _pipelined_gather(pos_hbm, y_hbm, buf_ref, pos_smem, pos_sem, row_sem, TOP_K * tm,
                             lambda step: TOP_K * tm)
    gates = gate_ref[...]
    z = alpha * _from_tiles(x_ref, tm)
    for kk in range(TOP_K):
        z = z + gates[:, kk:kk + 1] * _from_tiles(buf_ref.at[slot], tm, base=kk * tm * SUBLANES)
    o_ref[...] = _layer_norm(z, g_ref[...], b_ref[...])


def _combine_norm(y, pos, x1, gates, ln_g, ln_b, alpha, tm=512):
    d = SUBLANES * LANES
    n = x1.shape[0] // SUBLANES
    steps = n // tm
    pos_t = pos.reshape(steps, tm, TOP_K).transpose(0, 2, 1).reshape(steps, TOP_K * tm)
    row = lambda i: (i, 0)
    fixed = lambda i: (0, 0)
    return pl.pallas_call(
        functools.partial(_combine_kernel, tm=tm, alpha=alpha),
        out_shape=jax.ShapeDtypeStruct((n, d), F32),
        grid=(steps,),
        in_specs=[
            pl.BlockSpec(memory_space=pl.ANY),
            pl.BlockSpec(memory_space=pl.ANY),
            pl.BlockSpec((tm * SUBLANES, LANES), row),
            pl.BlockSpec((tm, 8), row),
            pl.BlockSpec((1, d), fixed),
            pl.BlockSpec((1, d), fixed),
        ],
        out_specs=pl.BlockSpec((tm, d), row),
        scratch_shapes=[pltpu.VMEM((2, TOP_K * tm * SUBLANES, LANES), F32),
                        pltpu.SMEM((2 * TOP_K * tm,), jnp.int32),
                        pltpu.SemaphoreType.DMA((2,)),
                        pltpu.SemaphoreType.DMA((2,))],
        compiler_params=_cparams("arbitrary"),
        name="combine_norm",
    )(pos_t, y, x1, gates, ln_g.reshape(1, d), ln_b.reshape(1, d))


def _routing_tables(top_idx, rank, hist, n):
    nk = n * TOP_K
    top_e = top_idx[:, :TOP_K]
    flat_e = top_e.reshape(nk)
    assign = jnp.arange(nk, dtype=jnp.int32)
    skey = lax.sort(flat_e * nk + assign)
    order = skey % nk
    counts = hist[-SUBLANES, :N_EXPERTS]
    start = jnp.cumsum(counts) - counts
    padded = (counts + MOE_BLOCK - 1) // MOE_BLOCK * MOE_BLOCK
    pad_end = jnp.cumsum(padded)
    pad_start = pad_end - padded
    pos = (pad_start[top_e] + rank[:, :TOP_K]).astype(jnp.int32)
    n_blocks = nk // MOE_BLOCK + N_EXPERTS + 1
    blk_first = jnp.arange(n_blocks, dtype=jnp.int32) * MOE_BLOCK
    block_e = jnp.sum((pad_end[None, :] <= blk_first[:, None]).astype(jnp.int32), axis=1)
    block_e = jnp.minimum(block_e, N_EXPERTS - 1)
    nvalid = jnp.clip(counts[block_e] - (blk_first - pad_start[block_e]), 0, MOE_BLOCK)
    nvalid = jnp.where(blk_first < pad_end[-1], nvalid, 0).astype(jnp.int32)
    lane = jnp.arange(MOE_BLOCK, dtype=jnp.int32)[None, :]
    first = (start[block_e] + blk_first - pad_start[block_e])[:, None]
    src = order[jnp.clip(first + lane, 0, nk - 1)] // TOP_K
    rows = jnp.where(lane < nvalid[:, None], src, 0).astype(jnp.int32)
    return rows, pos, block_e, nvalid


_EVEN_PLAN = (
    (512, "attn", HEAD_DIM ** -0.5), (128, "attn", 1.0), (128, None, 1.0),
    (512, None, 1.0), (512, None, 1.0), (512, None, 1.0), (512, None, 1.0),
)
_ODD_PLAN = (
    (256, "ret", 1.0), (256, "ret", C_KEY_DIM ** -0.5), (512, None, 1.0), (512, None, 1.0),
) + ((512, "attn", HEAD_DIM ** -0.5), (128, "attn", 1.0), (128, None, 1.0)) * len(D_PATTERNS)


def kernel(x, w_in_even, b_in_even, attn_sinks, hgrn_lb_logits, hgrn_norm, w_out_even, w_in_odd,
           b_in_odd, ret_norm, w_out_odd, ln1_g, ln1_b, ln2_g, ln2_b, router_w, router_b,
           expert_w_gu, expert_b_gu, expert_w_dn, expert_b_dn):
    batch, seq, d = x.shape
    n = batch * seq
    depth = ln1_g.shape[0]
    alpha = float((2 * depth) ** 0.25)

    rope_inv = 1.0 / (ROPE_THETA ** (jnp.arange(0, ROPE_DIM, 2, dtype=F32) / ROPE_DIM))
    ret_inv = 1.0 / (RET_THETA ** jnp.linspace(0.0, 1.0, C_KEY_DIM // 2, dtype=F32))
    tab_attn = _rope_tables(seq, ROPE_DIM // 2, rope_inv, HEAD_DIM)
    tab_ret = _rope_tables(seq, C_KEY_DIM // 2, ret_inv, C_KEY_DIM)
    lb_soft = jax.nn.softmax(hgrn_lb_logits.astype(F32), axis=0)
    lower_bounds = jnp.concatenate(
        [jnp.zeros_like(lb_soft[:1]), jnp.cumsum(lb_soft, axis=0)[:-1]], axis=0)

    xf = x.reshape(n, d)
    for layer in range(depth):
        j = layer // 2
        if layer % 2 == 0:
            aq, ak, av, bq, bf, bi, bg = _inproj(
                xf, w_in_even[j].astype(BF16), b_in_even[j].reshape(1, -1), tab_attn, tab_ret,
                _EVEN_PLAN, seq)
            (ya,) = _banded_attention(aq, ak, av, batch, seq, 1, A_WINDOW - 1,
                                      sinks=attn_sinks[j])
            yb = _hgrn2(bq, bf, bi, bg, lower_bounds[j], hgrn_norm[j], batch, seq)
            w_out = w_out_even[j]
        else:
            outs = _inproj(xf, w_in_odd[j].astype(BF16), b_in_odd[j].reshape(1, -1), tab_attn,
                           tab_ret, _ODD_PLAN, seq)
            cq, ck, cv, cg = outs[:4]
            ya = _retention(cq, ck, cv, cg, ret_norm[j], batch, seq)
            parts = []
            for p, (window, dil) in enumerate(D_PATTERNS):
                dq, dk, dv = outs[4 + 3 * p:7 + 3 * p]
                o, lse = _banded_attention(dq, dk, dv, batch, seq, dil, window // dil,
                                           want_lse=True, out_dtype=F32)
                parts += [o, lse]
            yb = _merge_patterns(parts)
            w_out = w_out_odd[j]
        x1, top_idx, gates, rank, hist = _outproj_norm_route(
            ya, yb, w_out, xf, ln1_g[layer], ln1_b[layer], router_w[layer], router_b[layer], alpha)
        rows, pos, block_e, nvalid = _routing_tables(top_idx, rank, hist, n)
        y = _moe_experts(x1, rows, block_e, nvalid, expert_w_gu, expert_b_gu, expert_w_dn,
                         expert_b_dn, layer)
        xf = _combine_norm(y, pos, x1, gates, ln2_g[layer], ln2_b[layer], alpha)
    return xf.reshape(batch, seq, d)
```

```python
import functools
import math

import jax
import jax.numpy as jnp
from jax import lax
from jax.experimental import pallas as pl
from jax.experimental.pallas import tpu as pltpu

F32 = jnp.float32
BF16 = jnp.bfloat16

HEAD_DIM = 64
ATTN_BLOCK = 128
A_Q_HEADS = 8
A_KV_HEADS = 2
A_WINDOW = 128
B_HEADS = 4
B_KEY_DIM = 128
B_MIN_F = 1e-30
C_HEADS = 4
C_KEY_DIM = 64
C_VAL_DIM = 128
D_PATTERNS = ((128, 1), (512, 4), (2048, 16))
ROPE_THETA = 500000.0
ROPE_DIM = HEAD_DIM // 4
RET_THETA = 10000.0
N_EXPERTS = 32
TOP_K = 4
SWIGLU_LIMIT = 7.0
SWIGLU_ALPHA = 1.702
MOE_BLOCK = 512
COMBINE_TILE = 256
LN_EPS = 1e-5
NORM_EPS = 1e-6
NEG_BIG = -1e30

LANES = 128
SUBLANES = 8
VMEM_LIMIT = 48 * 1024 * 1024
MOE_VMEM_LIMIT = 56 * 1024 * 1024

HGRN_CHUNK = 128
HGRN_SUB = 8
HGRN_LEVELS = (16, 32, 64, 128)


def _cparams(*sem):
    return pltpu.CompilerParams(dimension_semantics=sem, vmem_limit_bytes=VMEM_LIMIT)


def _to_tiles(ref, value):
    rows = value.shape[0]
    for c in range(SUBLANES):
        ref[pl.ds(c, rows, stride=SUBLANES), :] = value[:, c * LANES:(c + 1) * LANES]


def _from_tiles(ref, rows, base=0):
    return jnp.concatenate(
        [ref[pl.ds(base + c, rows, stride=SUBLANES), :] for c in range(SUBLANES)], axis=1)


def _sigmoid(z):
    return 1.0 / (1.0 + jnp.exp(-z))


def _rotate(h, cos, sa, sb, shift):
    width = h.shape[1]
    rep = width // LANES
    if rep > 1:
        cos = jnp.concatenate([cos] * rep, axis=1)
        sa = jnp.concatenate([sa] * rep, axis=1)
        sb = jnp.concatenate([sb] * rep, axis=1)
    up = pltpu.roll(h, width - shift, axis=1)
    dn = pltpu.roll(h, shift, axis=1)
    return h * cos + up * sa + dn * sb


def _inproj_kernel(x_ref, w_ref, b_ref, ta_ref, tr_ref, *out_refs, plan):
    xb = x_ref[...].astype(BF16)
    off = 0
    for o_ref, (width, rope, scale) in zip(out_refs, plan):
        h = jnp.dot(xb, w_ref[:, off:off + width], preferred_element_type=F32)
        h = h + b_ref[:, off:off + width]
        if rope == "attn":
            h = _rotate(h, ta_ref[0], ta_ref[1], ta_ref[2], ROPE_DIM // 2)
        elif rope == "ret":
            h = _rotate(h, tr_ref[0], tr_ref[1], tr_ref[2], C_KEY_DIM // 2)
        if scale != 1.0:
            h = h * scale
        o_ref[...] = h.astype(o_ref.dtype)
        off += width


def _inproj(x, w, b, tab_attn, tab_ret, plan, seq, tm=512):
    n, d = x.shape
    cols = w.shape[1]
    tpb = seq // tm
    kern = functools.partial(_inproj_kernel, plan=plan)
    outs = tuple(jax.ShapeDtypeStruct((n, wd), F32) for wd, _, _ in plan)
    return pl.pallas_call(
        kern,
        out_shape=outs,
        grid=(n // tm,),
        in_specs=[
            pl.BlockSpec((tm, d), lambda i: (i, 0)),
            pl.BlockSpec((d, cols), lambda i: (0, 0)),
            pl.BlockSpec((1, cols), lambda i: (0, 0)),
            pl.BlockSpec((3, tm, LANES), lambda i: (0, i % tpb, 0)),
            pl.BlockSpec((3, tm, LANES), lambda i: (0, i % tpb, 0)),
        ],
        out_specs=tuple(pl.BlockSpec((tm, wd), lambda i: (i, 0)) for wd, _, _ in plan),
        compiler_params=_cparams("parallel"),
        name="inproj",
    )(x, w, b, tab_attn, tab_ret)


def _rope_tables(seq, half, inv_freq, period):
    pos = jnp.arange(seq, dtype=F32)
    ang = pos[:, None] * inv_freq[None, :]
    cos, sin = jnp.cos(ang), jnp.sin(ang)
    pad = period - 2 * half
    ones = jnp.ones((seq, pad), F32)
    zeros = jnp.zeros((seq, pad), F32)
    zh = jnp.zeros((seq, half), F32)
    cosf = jnp.concatenate([cos, cos, ones], axis=1)
    sa = jnp.concatenate([-sin, zh, zeros], axis=1)
    sb = jnp.concatenate([zh, sin, zeros], axis=1)
    rep = LANES // period
    return jnp.stack([jnp.tile(t, (1, rep)) for t in (cosf, sa, sb)], axis=0)


def _attn_kernel(*refs, dil, max_dist, nsub, has_sink, want_lse):
    if has_sink:
        sink_ref, refs = refs[0], refs[1:]
    q_refs = refs[:4]
    k_ref, v_ref, kp_ref, vp_ref = refs[4:8]
    o_ref = refs[8]
    lse_ref = refs[9] if want_lse else None
    n_out = 2 if want_lse else 1
    stage = refs[8 + n_out:]
    blk = ATTN_BLOCK
    qi = lax.broadcasted_iota(jnp.int32, (4 * blk, 2 * blk), 0) % blk
    kj = lax.broadcasted_iota(jnp.int32, (4 * blk, 2 * blk), 1)
    dist = qi - kj + blk
    band = (dist >= 0) & (dist <= max_dist)
    first_kmin = jnp.where(pl.program_id(1) == 0, blk, 0)
    band_first = band & (kj >= first_kmin)
    row4 = lax.broadcasted_iota(jnp.int32, (4 * blk, 1), 0) // blk

    def take(start):
        return pl.ds(start, blk, stride=dil) if dil > 1 else pl.ds(start, blk)

    def one_group(r, j):
        rows = take(j * blk * dil + r)
        if j == 0:
            k_prev, v_prev = kp_ref[take(r), :], vp_ref[take(r), :]
            valid = band_first
        else:
            prow = take((j - 1) * blk * dil + r)
            k_prev, v_prev = k_ref[prow, :], v_ref[prow, :]
            valid = band
        k_cat = jnp.concatenate([k_prev, k_ref[rows, :]], axis=0).astype(BF16)
        v_cat = jnp.concatenate([v_prev, v_ref[rows, :]], axis=0).astype(BF16)
        for h in range(A_KV_HEADS):
            qa, qb = q_refs[2 * h][rows, :], q_refs[2 * h + 1][rows, :]
            q4 = jnp.concatenate([qa[:, :HEAD_DIM], qa[:, HEAD_DIM:], qb[:, :HEAD_DIM],
                                  qb[:, HEAD_DIM:]], axis=0).astype(BF16)
            kh = k_cat[:, h * HEAD_DIM:(h + 1) * HEAD_DIM]
            vh = v_cat[:, h * HEAD_DIM:(h + 1) * HEAD_DIM]
            s = lax.dot_general(q4, kh, (((1,), (1,)), ((), ())), preferred_element_type=F32)
            s = jnp.where(valid, s, NEG_BIG)
            m = jnp.max(s, axis=1, keepdims=True)
            if has_sink:
                sk = jnp.full((4 * blk, 1), sink_ref[4 * h + 3], F32)
                for g in range(3):
                    sk = jnp.where(row4 == g, sink_ref[4 * h + g], sk)
                m = jnp.maximum(m, sk)
            p = jnp.exp(s - m)
            den = jnp.sum(p, axis=1, keepdims=True)
            if has_sink:
                den = den + jnp.exp(sk - m)
            o = jnp.dot(p.astype(BF16), vh, preferred_element_type=F32) / den
            lse = m + jnp.log(den) if want_lse else None
            for half in range(2):
                cb = 2 * h + half
                g0 = 2 * half
                pair = jnp.concatenate(
                    [o[g0 * blk:(g0 + 1) * blk], o[(g0 + 1) * blk:(g0 + 2) * blk]], axis=1)
                if dil > 1:
                    stage[0][cb, rows, :] = pair
                else:
                    o_ref[rows, cb * LANES:(cb + 1) * LANES] = pair.astype(o_ref.dtype)
                if want_lse:
                    lpair = jnp.concatenate(
                        [jnp.broadcast_to(lse[g * blk:(g + 1) * blk], (blk, HEAD_DIM))
                         for g in (g0, g0 + 1)], axis=1)
                    if dil > 1:
                        stage[1][cb, rows, :] = lpair
                    else:
                        lse_ref[rows, cb * LANES:(cb + 1) * LANES] = lpair

    for j in range(nsub):
        if dil == 1:
            one_group(0, j)
        else:
            def body(r, carry, j=j):
                one_group(r, j)
                return carry
            lax.fori_loop(0, dil, body, 0)
    if dil > 1:
        for cb in range(4):
            o_ref[:, cb * LANES:(cb + 1) * LANES] = stage[0][cb].astype(o_ref.dtype)
            if want_lse:
                lse_ref[:, cb * LANES:(cb + 1) * LANES] = stage[1][cb]


def _banded_attention(q, k, v, batch, seq, dil, max_dist, sinks=None, want_lse=False,
                      out_dtype=BF16):
    span = ATTN_BLOCK * dil
    nsub = max(1, 512 // span)
    tb = span * nsub
    nt = seq // tb
    qw, kw = q.shape[1], k.shape[1]
    q3, k3, v3 = (t.reshape(batch, seq, t.shape[1]) for t in (q, k, v))
    kern = functools.partial(_attn_kernel, dil=dil, max_dist=max_dist, nsub=nsub,
                             has_sink=sinks is not None, want_lse=want_lse)
    cur = lambda b, t: (b, t, 0)
    prev = lambda b, t: (b, jnp.maximum(t * nsub - 1, 0), 0)
    in_specs = [pl.BlockSpec((None, tb, LANES), lambda b, t, c=c: (b, t, c)) for c in range(4)] + [
        pl.BlockSpec((None, tb, kw), cur),
        pl.BlockSpec((None, tb, kw), cur),
        pl.BlockSpec((None, span, kw), prev),
        pl.BlockSpec((None, span, kw), prev),
    ]
    args = [q3, q3, q3, q3, k3, v3, k3, v3]
    if sinks is not None:
        in_specs = [pl.BlockSpec(memory_space=pltpu.SMEM)] + in_specs
        args = [sinks.astype(F32)] + args
    n_out = 2 if want_lse else 1
    out_shape = [jax.ShapeDtypeStruct((batch, seq, qw), out_dtype)]
    out_specs = [pl.BlockSpec((None, tb, qw), cur)]
    if want_lse:
        out_shape.append(jax.ShapeDtypeStruct((batch, seq, qw), F32))
        out_specs.append(pl.BlockSpec((None, tb, qw), cur))
    scratch = [pltpu.VMEM((4, tb, LANES), F32)] * n_out if dil > 1 else []
    res = pl.pallas_call(
        kern,
        out_shape=tuple(out_shape),
        grid=(batch, nt),
        in_specs=in_specs,
        out_specs=tuple(out_specs),
        scratch_shapes=scratch,
        compiler_params=_cparams("parallel", "arbitrary"),
        name=f"banded_attn_d{dil}",
    )(*args)
    return tuple(r.reshape(batch * seq, qw) for r in res)


def _merge_kernel(o1, l1, o2, l2, o3, l3, out_ref):
    a, b, c = l1[...], l2[...], l3[...]
    m = jnp.maximum(jnp.maximum(a, b), c)
    ea, eb, ec = jnp.exp(a - m), jnp.exp(b - m), jnp.exp(c - m)
    num = ea * o1[...] + eb * o2[...] + ec * o3[...]
    out_ref[...] = (num / (ea + eb + ec)).astype(out_ref.dtype)


def _merge_patterns(parts, tm=512):
    n, w = parts[0].shape
    spec = pl.BlockSpec((tm, w), lambda i: (i, 0))
    return pl.pallas_call(
        _merge_kernel,
        out_shape=jax.ShapeDtypeStruct((n, w), BF16),
        grid=(n // tm,),
        in_specs=[spec] * 6,
        out_specs=spec,
        compiler_params=_cparams("parallel"),
        name="merge_patterns",
    )(*parts)


def _hgrn_structure():
    c = HGRN_CHUNK
    r = lax.broadcasted_iota(jnp.int32, (c, c), 0)
    j = lax.broadcasted_iota(jnp.int32, (c, c), 1)
    blocks = [(j <= r)]
    for lv in HGRN_LEVELS:
        mid = (r // lv) * lv + lv // 2 - 1
        second = (r % lv) >= lv // 2
        lo = jnp.where(second, mid, r)
        hi = jnp.where(second, r, mid)
        blocks.append((j > lo) & (j <= hi))
    return jnp.concatenate([jnp.where(m, 1.0, 0.0) for m in blocks], axis=0).astype(BF16)


def _split3(x):
    hi = x.astype(BF16)
    r1 = x - hi.astype(F32)
    mid = r1.astype(BF16)
    lo = (r1 - mid.astype(F32)).astype(BF16)
    return hi, mid, lo


def _hgrn_kernel(q_ref, f_ref, i_ref, g_ref, lb_ref, gain_ref, o_ref, st_ref, *, nchunk):
    c = HGRN_CHUNK

    @pl.when(pl.program_id(2) == 0)
    def _():
        st_ref[...] = jnp.zeros_like(st_ref)

    dmat = _hgrn_structure()
    row = lax.broadcasted_iota(jnp.int32, (c, c), 0)
    col = lax.broadcasted_iota(jnp.int32, (c, c), 1)
    lb = jnp.clip(lb_ref[...], 0.0, 1.0)
    gain = gain_ref[...]
    sub_row = lax.broadcasted_iota(jnp.int32, (HGRN_SUB, c), 0)
    sub_col = lax.broadcasted_iota(jnp.int32, (HGRN_SUB, c), 1)

    for ci in range(nchunk):
        rows = pl.ds(ci * c, c)
        q = q_ref[rows, :]
        f = lb + (1.0 - lb) * _sigmoid(f_ref[rows, :])
        lf = jnp.log(jnp.maximum(f, B_MIN_F))
        key = 1.0 - f
        val = i_ref[rows, :].astype(BF16)

        hi, mid, lo = _split3(lf)
        sums = (jnp.dot(dmat, hi, preferred_element_type=F32)
                + jnp.dot(dmat, mid, preferred_element_type=F32)
                + jnp.dot(dmat, lo, preferred_element_type=F32))
        b = sums[0:c]

        att = jnp.zeros((c, c), F32)
        for n, lv in enumerate(HGRN_LEVELS):
            e = jnp.exp(sums[(n + 1) * c:(n + 2) * c])
            second = (row % lv) >= lv // 2
            qs = jnp.where(second, q * e, 0.0).astype(BF16)
            ks = jnp.where(second, 0.0, key * e).astype(BF16)
            a = lax.dot_general(qs, ks, (((1,), (1,)), ((), ())), preferred_element_type=F32)
            att = att + jnp.where((row // lv) == (col // lv), a, 0.0)

        diag = []
        for a0 in range(0, c, HGRN_SUB):
            qa, ka, ba = q[a0:a0 + HGRN_SUB], key[a0:a0 + HGRN_SUB], b[a0:a0 + HGRN_SUB]
            acc = jnp.zeros((HGRN_SUB, c), F32)
            for s in range(HGRN_SUB):
                e = jnp.exp(jnp.minimum(ba - ba[s:s + 1, :], 0.0))
                w = jnp.sum(qa * e * ka[s:s + 1, :], axis=1, keepdims=True)
                acc = jnp.where((sub_col == a0 + s) & (sub_row >= s), w, acc)
            diag.append(acc)
        att = att + jnp.concatenate(diag, axis=0)

        st = st_ref[...]
        b_last = b[c - 1:c, :]
        q_in = (q * jnp.exp(b)).astype(BF16)
        k_out = (key * jnp.exp(b_last - b)).astype(BF16)
        o = jnp.dot(att.astype(BF16), val, preferred_element_type=F32)
        o = o + lax.dot_general(q_in, st.astype(BF16), (((1,), (1,)), ((), ())),
                                preferred_element_type=F32)
        st_ref[...] = jnp.exp(b_last) * st + lax.dot_general(
            val, k_out, (((0,), (0,)), ((), ())), preferred_element_type=F32)

        o = o * lax.rsqrt(jnp.mean(o * o, axis=1, keepdims=True) + NORM_EPS)
        g = g_ref[rows, :]
        o_ref[rows, :] = (o * gain * (g * _sigmoid(g))).astype(o_ref.dtype)


def _hgrn2(q, f, i_in, g, lb, gain, batch, seq, nchunk=4):
    width = q.shape[1]
    heads = width // B_KEY_DIM
    tb = HGRN_CHUNK * nchunk
    q3, f3, i3, g3 = (t.reshape(batch, seq, width) for t in (q, f, i_in, g))
    blk = pl.BlockSpec((None, tb, B_KEY_DIM), lambda b, h, t: (b, t, h))
    vec = pl.BlockSpec((1, B_KEY_DIM), lambda b, h, t: (0, h))
    out = pl.pallas_call(
        functools.partial(_hgrn_kernel, nchunk=nchunk),
        out_shape=jax.ShapeDtypeStruct((batch, seq, width), BF16),
        grid=(batch, heads, seq // tb),
        in_specs=[blk, blk, blk, blk, vec, vec],
        out_specs=blk,
        scratch_shapes=[pltpu.VMEM((B_KEY_DIM, B_KEY_DIM), F32)],
        compiler_params=_cparams("parallel", "parallel", "arbitrary"),
        name="hgrn2",
    )(q3, f3, i3, g3, lb.reshape(1, width).astype(F32), gain.reshape(1, width).astype(F32))
    return out.reshape(batch * seq, width)


def _retention_kernel(q_ref, k_ref, v_ref, g_ref, gain_ref, o_ref, st_ref, *, nchunk):
    c = ATTN_BLOCK

    @pl.when(pl.program_id(1) == 0)
    def _():
        st_ref[...] = jnp.zeros_like(st_ref)

    ri = lax.broadcasted_iota(jnp.int32, (c, c), 0)
    ci_ = lax.broadcasted_iota(jnp.int32, (c, c), 1)
    rel = (ri - ci_).astype(F32)
    rowf = ri.astype(F32)

    for h in range(C_HEADS):
        log_gamma = math.log1p(-(2.0 ** (-5.0 - h)))
        decay = jnp.where(rel >= 0.0, jnp.exp(log_gamma * jnp.maximum(rel, 0.0)), 0.0)
        head_scale = jnp.exp(log_gamma * (rowf + 1.0))
        tail_scale = jnp.exp(log_gamma * (c - 1.0 - rowf))
        chunk_decay = math.exp(log_gamma * c)
        gain = gain_ref[:, h * C_VAL_DIM:(h + 1) * C_VAL_DIM]
        for ci in range(nchunk):
            rows = pl.ds(ci * c, c)
            q = q_ref[rows, h * C_KEY_DIM:(h + 1) * C_KEY_DIM].astype(BF16)
            k = k_ref[rows, h * C_KEY_DIM:(h + 1) * C_KEY_DIM].astype(BF16)
            v = v_ref[rows, h * C_VAL_DIM:(h + 1) * C_VAL_DIM]
            st = st_ref[h]
            s = lax.dot_general(q, k, (((1,), (1,)), ((), ())), preferred_element_type=F32)
            o = jnp.dot((s * decay).astype(BF16), v.astype(BF16), preferred_element_type=F32)
            o = o + head_scale * jnp.dot(q, st.astype(BF16), preferred_element_type=F32)
            st_ref[h] = chunk_decay * st + lax.dot_general(
                k, (v * tail_scale).astype(BF16), (((0,), (0,)), ((), ())),
                preferred_element_type=F32)
            o = o - jnp.mean(o, axis=1, keepdims=True)
            o = o * lax.rsqrt(jnp.mean(o * o, axis=1, keepdims=True) + NORM_EPS)
            g = g_ref[rows, h * C_VAL_DIM:(h + 1) * C_VAL_DIM]
            o_ref[rows, h * C_VAL_DIM:(h + 1) * C_VAL_DIM] = (
                o * gain * (g * _sigmoid(g))).astype(o_ref.dtype)


def _retention(q, k, v, g, gain, batch, seq, nchunk=4):
    tb = ATTN_BLOCK * nchunk
    kw, vw = q.shape[1], v.shape[1]
    q3, k3 = q.reshape(batch, seq, kw), k.reshape(batch, seq, kw)
    v3, g3 = v.reshape(batch, seq, vw), g.reshape(batch, seq, vw)
    kspec = pl.BlockSpec((None, tb, kw), lambda b, t: (b, t, 0))
    vspec = pl.BlockSpec((None, tb, vw), lambda b, t: (b, t, 0))
    out = pl.pallas_call(
        functools.partial(_retention_kernel, nchunk=nchunk),
        out_shape=jax.ShapeDtypeStruct((batch, seq, vw), BF16),
        grid=(batch, seq // tb),
        in_specs=[kspec, kspec, vspec, vspec, pl.BlockSpec((1, vw), lambda b, t: (0, 0))],
        out_specs=vspec,
        scratch_shapes=[pltpu.VMEM((C_HEADS, C_KEY_DIM, C_VAL_DIM), F32)],
        compiler_params=_cparams("parallel", "arbitrary"),
        name="retention",
    )(q3, k3, v3, g3, gain.reshape(1, vw).astype(F32))
    return out.reshape(batch * seq, vw)


def _layer_norm(z, g, b):
    mu = jnp.mean(z, axis=1, keepdims=True)
    zc = z - mu
    var = jnp.mean(zc * zc, axis=1, keepdims=True)
    return zc * lax.rsqrt(var + LN_EPS) * g + b


def _outproj_kernel(ya_ref, yb_ref, w_ref, x_ref, g_ref, b_ref, rwh_ref, rwl_ref, rb_ref,
                    x1_ref, idx_ref, gate_ref, hist_ref, run_ref, *, alpha):
    @pl.when(pl.program_id(0) == 0)
    def _():
        run_ref[...] = jnp.zeros_like(run_ref)

    half = ya_ref.shape[1]
    mix = jnp.dot(ya_ref[...], w_ref[0:half, :], preferred_element_type=F32)
    mix = mix + jnp.dot(yb_ref[...], w_ref[half:, :], preferred_element_type=F32)
    x1 = _layer_norm(alpha * x_ref[...] + mix, g_ref[...], b_ref[...])
    _to_tiles(x1_ref, x1)

    xh = x1.astype(BF16)
    xl = (x1 - xh.astype(F32)).astype(BF16)
    logits = (jnp.dot(xh, rwh_ref[...], preferred_element_type=F32)
              + jnp.dot(xh, rwl_ref[...], preferred_element_type=F32)
              + jnp.dot(xl, rwh_ref[...], preferred_element_type=F32)) + rb_ref[...]
    lane = lax.broadcasted_iota(jnp.int32, logits.shape, 1)
    idx_acc = jnp.zeros(logits.shape, jnp.int32)
    val_acc = jnp.zeros(logits.shape, F32)
    top0 = None
    den = None
    for kk in range(TOP_K):
        m = jnp.max(logits, axis=1, keepdims=True)
        sel = jnp.min(jnp.where(logits == m, lane, LANES), axis=1, keepdims=True)
        if kk == 0:
            top0 = m
        e = jnp.exp(m - top0)
        den = e if den is None else den + e
        idx_acc = jnp.where(lane == kk, sel, idx_acc)
        val_acc = jnp.where(lane == kk, e, val_acc)
        picked = lane == sel
        onehot = jnp.where(picked, 1.0, 0.0)
        chosen = onehot if kk == 0 else chosen + onehot
        logits = jnp.where(picked, NEG_BIG * 2.0, logits)
    idx_ref[...] = idx_acc[:, 0:idx_ref.shape[1]]
    gate_ref[...] = (val_acc / den)[:, 0:gate_ref.shape[1]]

    run_ref[...] = run_ref[...] + jnp.sum(chosen, axis=0, keepdims=True)
    hist_ref[...] = jnp.broadcast_to(run_ref[...], hist_ref.shape).astype(jnp.int32)


def _outproj_norm_route(ya, yb, w_out, x, ln_g, ln_b, rw, rb, alpha, tm=512):
    n, d = x.shape
    half = ya.shape[1]
    rw_pad = jnp.zeros((d, LANES), F32).at[:, :N_EXPERTS].set(rw.astype(F32))
    rwh = rw_pad.astype(BF16)
    rwl = (rw_pad - rwh.astype(F32)).astype(BF16)
    rb_pad = jnp.full((1, LANES), NEG_BIG, F32).at[0, :N_EXPERTS].set(rb.astype(F32))
    row = lambda i: (i, 0)
    fixed = lambda i: (0, 0)
    return pl.pallas_call(
        functools.partial(_outproj_kernel, alpha=alpha),
        out_shape=(jax.ShapeDtypeStruct((n * SUBLANES, LANES), F32),
                   jax.ShapeDtypeStruct((n, 8), jnp.int32),
                   jax.ShapeDtypeStruct((n, 8), F32),
                   jax.ShapeDtypeStruct((n // tm * SUBLANES, LANES), jnp.int32)),
        grid=(n // tm,),
        in_specs=[
            pl.BlockSpec((tm, half), row),
            pl.BlockSpec((tm, half), row),
            pl.BlockSpec((2 * half, d), fixed),
            pl.BlockSpec((tm, d), row),
            pl.BlockSpec((1, d), fixed),
            pl.BlockSpec((1, d), fixed),
            pl.BlockSpec((d, LANES), fixed),
            pl.BlockSpec((d, LANES), fixed),
            pl.BlockSpec((1, LANES), fixed),
        ],
        out_specs=(pl.BlockSpec((tm * SUBLANES, LANES), row),
                   pl.BlockSpec((tm, 8), row),
                   pl.BlockSpec((tm, 8), row),
                   pl.BlockSpec((SUBLANES, LANES), row)),
        scratch_shapes=[pltpu.VMEM((1, LANES), F32)],
        compiler_params=_cparams("arbitrary"),
        name="outproj_norm_route",
    )(ya, yb, w_out.astype(BF16), x, ln_g.reshape(1, d), ln_b.reshape(1, d), rwh, rwl, rb_pad)


ISSUE_UNROLL = 8
DRAIN_UNROLL = 16


def _issue_tiles(idx_smem, base, src_hbm, dst_ref, sem, count):
    def body(g, carry):
        for u in range(ISSUE_UNROLL):
            r = g * ISSUE_UNROLL + u
            src = pl.multiple_of(idx_smem[base + r] * SUBLANES, SUBLANES)
            dst = pl.multiple_of(r * SUBLANES, SUBLANES)
            pltpu.make_async_copy(src_hbm.at[pl.ds(src, SUBLANES), :],
                                  dst_ref.at[pl.ds(dst, SUBLANES), :], sem).start(priority=u % 2)
        return carry

    lax.fori_loop(0, count // ISSUE_UNROLL, body, 0)


def _drain_tiles(src_hbm, dst_ref, sem, count):
    def body(r, carry):
        dst = pl.multiple_of(r * SUBLANES, SUBLANES)
        pltpu.make_async_copy(src_hbm.at[pl.ds(0, SUBLANES), :],
                              dst_ref.at[pl.ds(dst, SUBLANES), :], sem).wait()
        return carry

    def group(g, carry):
        for u in range(DRAIN_UNROLL):
            body(g * DRAIN_UNROLL + u, carry)
        return carry

    lax.fori_loop(0, count // DRAIN_UNROLL, group, 0)


def _scatter_tiles(idx_smem, base, src_ref, dst_hbm, sem, count):
    def body(g, carry):
        for u in range(ISSUE_UNROLL):
            r = g * ISSUE_UNROLL + u
            dst = pl.multiple_of(idx_smem[base + r] * SUBLANES, SUBLANES)
            src = pl.multiple_of(r * SUBLANES, SUBLANES)
            pltpu.make_async_copy(src_ref.at[pl.ds(src, SUBLANES), :],
                                  dst_hbm.at[pl.ds(dst, SUBLANES), :], sem).start(priority=u % 2)
        return carry

    lax.fori_loop(0, count // ISSUE_UNROLL, body, 0)


def _drain_scatter(src_ref, dst_hbm, sem, count):
    def group(g, carry):
        for u in range(DRAIN_UNROLL):
            src = pl.multiple_of((g * DRAIN_UNROLL + u) * SUBLANES, SUBLANES)
            pltpu.make_async_copy(src_ref.at[pl.ds(src, SUBLANES), :],
                                  dst_hbm.at[pl.ds(0, SUBLANES), :], sem).wait()
        return carry

    lax.fori_loop(0, count // DRAIN_UNROLL, group, 0)


def _pipelined_gather(idx_hbm, src_hbm, buf, idx_smem, idx_sem, row_sem, width, rows_of):
    i = pl.program_id(0)
    n = pl.num_programs(0)
    slot = i % 2
    nslot = 1 - slot

    def idx_copy(step, s):
        return pltpu.make_async_copy(idx_hbm.at[step], idx_smem.at[pl.ds(s * width, width)],
                                     idx_sem.at[s])

    def issue(step, s):
        _issue_tiles(idx_smem, s * width, src_hbm, buf.at[s], row_sem.at[s], rows_of(step))

    @pl.when(i == 0)
    def _():
        idx_copy(0, 0).start()
        idx_copy(0, 0).wait()
        issue(0, 0)

        @pl.when(n > 1)
        def _():
            idx_copy(1, 1).start()

    @pl.when(i + 1 < n)
    def _():
        idx_copy(i + 1, nslot).wait()
        issue(i + 1, nslot)

    def prefetch_index_row():
        @pl.when(i + 2 < n)
        def _():
            idx_copy(i + 2, slot).start()

    _drain_tiles(src_hbm, buf.at[slot], row_sem.at[slot], rows_of(i))
    return slot, prefetch_index_row


def _moe_kernel(be_ref, nv_ref, tab_hbm, x_hbm, wgu_ref, bgu_ref, wdn_ref, bdn_ref, y_hbm,
                xbuf, ybuf, idx_smem, idx_sem, row_sem, out_sem, act_ref, wgu_bf, wdn_bf, *, chunk):
    i = pl.program_id(0)
    n = pl.num_programs(0)
    de = wdn_ref.shape[0]
    width = 2 * MOE_BLOCK

    def rows_of(step):
        return (nv_ref[step] + DRAIN_UNROLL - 1) // DRAIN_UNROLL * DRAIN_UNROLL

    @pl.when(i == 0)
    def _():
        xbuf[...] = jnp.zeros_like(xbuf)
        spare0 = y_hbm.shape[0] - 2 * MOE_BLOCK * SUBLANES
        for h in range(2):
            fill = pltpu.make_async_copy(
                xbuf.at[h], y_hbm.at[pl.ds(spare0 + h * MOE_BLOCK * SUBLANES, MOE_BLOCK * SUBLANES), :],
                out_sem.at[h])
            fill.start()
            fill.wait()

    slot, prefetch_index_row = _pipelined_gather(tab_hbm, x_hbm, xbuf, idx_smem, idx_sem, row_sem,
                                                 width, rows_of)

    @pl.when(i >= 2)
    def _():
        _drain_scatter(ybuf.at[slot], y_hbm, out_sem.at[slot], rows_of(jnp.maximum(i - 2, 0)))

    @pl.when((i == 0) | (be_ref[i] != be_ref[jnp.maximum(i - 1, 0)]))
    def _():
        for c0 in range(0, 2 * de, chunk):
            wgu_bf[:, c0:c0 + chunk] = wgu_ref[:, c0:c0 + chunk].astype(BF16)
        for c0 in range(0, wdn_ref.shape[1], chunk):
            wdn_bf[:, c0:c0 + chunk] = wdn_ref[:, c0:c0 + chunk].astype(BF16)

    @pl.when(nv_ref[i] > 0)
    def _():
        xb = _from_tiles(xbuf.at[slot], MOE_BLOCK).astype(BF16)
        for c0 in range(0, de, chunk):
            gate = jnp.dot(xb, wgu_bf[:, c0:c0 + chunk], preferred_element_type=F32)
            gate = gate + bgu_ref[:, c0:c0 + chunk]
            up = jnp.dot(xb, wgu_bf[:, de + c0:de + c0 + chunk], preferred_element_type=F32)
            up = up + bgu_ref[:, de + c0:de + c0 + chunk]
            gate = jnp.minimum(gate, SWIGLU_LIMIT)
            up = jnp.clip(up, -SWIGLU_LIMIT, SWIGLU_LIMIT)
            act = (up + 1.0) * gate * _sigmoid(SWIGLU_ALPHA * gate)
            act_ref[:, c0:c0 + chunk] = act.astype(BF16)
        _to_tiles(ybuf.at[slot], jnp.dot(act_ref[...], wdn_bf[...], preferred_element_type=F32)
                  + bdn_ref[...])

    _scatter_tiles(idx_smem, slot * width + MOE_BLOCK, ybuf.at[slot], y_hbm, out_sem.at[slot],
                   rows_of(i))
    prefetch_index_row()

    @pl.when(i == n - 1)
    def _():
        @pl.when(n >= 2)
        def _():
            _drain_scatter(ybuf.at[1 - slot], y_hbm, out_sem.at[1 - slot],
                           rows_of(jnp.maximum(i - 1, 0)))

        _drain_scatter(ybuf.at[slot], y_hbm, out_sem.at[slot], rows_of(i))


def _moe_experts(x1, table, block_e, nvalid, w_gu, b_gu, w_dn, b_dn, layer, out_tiles, chunk=512):
    n_exp, de, d = w_dn.shape[1:]
    assert d == SUBLANES * LANES
    nblocks = table.shape[0]
    tile_rows = MOE_BLOCK * SUBLANES
    grid_spec = pltpu.PrefetchScalarGridSpec(
        num_scalar_prefetch=2,
        grid=(nblocks,),
        in_specs=[
            pl.BlockSpec(memory_space=pl.ANY),
            pl.BlockSpec(memory_space=pl.ANY),
            pl.BlockSpec((None, None, d, 2 * de), lambda i, be, nv: (layer, be[i], 0, 0)),
            pl.BlockSpec((None, None, 1, 2 * de), lambda i, be, nv: (layer, be[i], 0, 0)),
            pl.BlockSpec((None, None, de, d), lambda i, be, nv: (layer, be[i], 0, 0)),
            pl.BlockSpec((None, None, 1, d), lambda i, be, nv: (layer, be[i], 0, 0)),
        ],
        out_specs=pl.BlockSpec(memory_space=pl.ANY),
        scratch_shapes=[pltpu.VMEM((2, tile_rows, LANES), F32),
                        pltpu.VMEM((2, tile_rows, LANES), F32),
                        pltpu.SMEM((4 * MOE_BLOCK,), jnp.int32),
                        pltpu.SemaphoreType.DMA((2,)),
                        pltpu.SemaphoreType.DMA((2,)),
                        pltpu.SemaphoreType.DMA((2,)),
                        pltpu.VMEM((MOE_BLOCK, de), BF16),
                        pltpu.VMEM((d, 2 * de), BF16),
                        pltpu.VMEM((de, d), BF16)],
    )
    depth = w_dn.shape[0]
    return pl.pallas_call(
        functools.partial(_moe_kernel, chunk=chunk),
        out_shape=jax.ShapeDtypeStruct((out_tiles * SUBLANES, LANES), F32),
        grid_spec=grid_spec,
        compiler_params=pltpu.CompilerParams(dimension_semantics=("arbitrary",),
                                             vmem_limit_bytes=MOE_VMEM_LIMIT),
        name="moe_experts",
    )(block_e, nvalid, table, x1, w_gu, b_gu.reshape(depth, n_exp, 1, 2 * de),
      w_dn, b_dn.reshape(depth, n_exp, 1, d))


def _combine_kernel(y_ref, x_ref, gate_ref, g_ref, b_ref, o_ref, *, tm, alpha):
    gates = gate_ref[...]
    z = alpha * _from_tiles(x_ref, tm)
    for kk in range(TOP_K):
        yk = jnp.concatenate(
            [y_ref[pl.ds(kk * SUBLANES + c, tm, stride=TOP_K * SUBLANES), :] for c in range(SUBLANES)],
            axis=1)
        z = z + gates[:, kk:kk + 1] * yk
    o_ref[...] = _layer_norm(z, g_ref[...], b_ref[...])


def _combine_norm(y, x1, gates, ln_g, ln_b, alpha, tm=256):
    d = SUBLANES * LANES
    n = x1.shape[0] // SUBLANES
    row = lambda i: (i, 0)
    fixed = lambda i: (0, 0)
    return pl.pallas_call(
        functools.partial(_combine_kernel, tm=tm, alpha=alpha),
        out_shape=jax.ShapeDtypeStruct((n, d), F32),
        grid=(n // tm,),
        in_specs=[
            pl.BlockSpec((tm * TOP_K * SUBLANES, LANES), row),
            pl.BlockSpec((tm * SUBLANES, LANES), row),
            pl.BlockSpec((tm, 8), row),
            pl.BlockSpec((1, d), fixed),
            pl.BlockSpec((1, d), fixed),
        ],
        out_specs=pl.BlockSpec((tm, d), row),
        compiler_params=_cparams("parallel"),
        name="combine_norm",
    )(y, x1, gates, ln_g.reshape(1, d), ln_b.reshape(1, d))


def _routing_tables(top_idx, hist, n):
    nk = n * TOP_K
    flat_e = top_idx[:, :TOP_K].reshape(nk)
    assign = jnp.arange(nk, dtype=jnp.int32)
    skey = lax.sort(flat_e * nk + assign)
    order = skey % nk
    counts = hist[-SUBLANES, :N_EXPERTS]
    start = jnp.cumsum(counts) - counts
    padded = (counts + MOE_BLOCK - 1) // MOE_BLOCK * MOE_BLOCK
    pad_end = jnp.cumsum(padded)
    pad_start = pad_end - padded
    n_blocks = nk // MOE_BLOCK + N_EXPERTS + 1
    blk_first = jnp.arange(n_blocks, dtype=jnp.int32) * MOE_BLOCK
    block_e = jnp.sum((pad_end[None, :] <= blk_first[:, None]).astype(jnp.int32), axis=1)
    block_e = jnp.minimum(block_e, N_EXPERTS - 1)
    nvalid = jnp.clip(counts[block_e] - (blk_first - pad_start[block_e]), 0, MOE_BLOCK)
    nvalid = jnp.where(blk_first < pad_end[-1], nvalid, 0).astype(jnp.int32)
    lane = jnp.arange(MOE_BLOCK, dtype=jnp.int32)[None, :]
    first = (start[block_e] + blk_first - pad_start[block_e])[:, None]
    held = order[jnp.clip(first + lane, 0, nk - 1)]
    live = lane < nvalid[:, None]
    spare = nk + (jnp.arange(n_blocks, dtype=jnp.int32) % 2)[:, None] * MOE_BLOCK + lane
    table = jnp.concatenate([jnp.where(live, held // TOP_K, 0), jnp.where(live, held, spare)], axis=1)
    return table.astype(jnp.int32), block_e, nvalid


_EVEN_PLAN = (
    (512, "attn", HEAD_DIM ** -0.5), (128, "attn", 1.0), (128, None, 1.0),
    (512, None, 1.0), (512, None, 1.0), (512, None, 1.0), (512, None, 1.0),
)
_ODD_PLAN = (
    (256, "ret", 1.0), (256, "ret", C_KEY_DIM ** -0.5), (512, None, 1.0), (512, None, 1.0),
) + ((512, "attn", HEAD_DIM ** -0.5), (128, "attn", 1.0), (128, None, 1.0)) * len(D_PATTERNS)


def kernel(x, w_in_even, b_in_even, attn_sinks, hgrn_lb_logits, hgrn_norm, w_out_even, w_in_odd,
           b_in_odd, ret_norm, w_out_odd, ln1_g, ln1_b, ln2_g, ln2_b, router_w, router_b,
           expert_w_gu, expert_b_gu, expert_w_dn, expert_b_dn):
    batch, seq, d = x.shape
    n = batch * seq
    depth = ln1_g.shape[0]
    alpha = float((2 * depth) ** 0.25)

    rope_inv = 1.0 / (ROPE_THETA ** (jnp.arange(0, ROPE_DIM, 2, dtype=F32) / ROPE_DIM))
    ret_inv = 1.0 / (RET_THETA ** jnp.linspace(0.0, 1.0, C_KEY_DIM // 2, dtype=F32))
    tab_attn = _rope_tables(seq, ROPE_DIM // 2, rope_inv, HEAD_DIM)
    tab_ret = _rope_tables(seq, C_KEY_DIM // 2, ret_inv, C_KEY_DIM)
    lb_soft = jax.nn.softmax(hgrn_lb_logits.astype(F32), axis=0)
    lower_bounds = jnp.concatenate(
        [jnp.zeros_like(lb_soft[:1]), jnp.cumsum(lb_soft, axis=0)[:-1]], axis=0)

    xf = x.reshape(n, d)
    for layer in range(depth):
        j = layer // 2
        if layer % 2 == 0:
            aq, ak, av, bq, bf, bi, bg = _inproj(
                xf, w_in_even[j].astype(BF16), b_in_even[j].reshape(1, -1), tab_attn, tab_ret,
                _EVEN_PLAN, seq)
            (ya,) = _banded_attention(aq, ak, av, batch, seq, 1, A_WINDOW - 1,
                                      sinks=attn_sinks[j])
            yb = _hgrn2(bq, bf, bi, bg, lower_bounds[j], hgrn_norm[j], batch, seq)
            w_out = w_out_even[j]
        else:
            outs = _inproj(xf, w_in_odd[j].astype(BF16), b_in_odd[j].reshape(1, -1), tab_attn,
                           tab_ret, _ODD_PLAN, seq)
            cq, ck, cv, cg = outs[:4]
            ya = _retention(cq, ck, cv, cg, ret_norm[j], batch, seq)
            parts = []
            for p, (window, dil) in enumerate(D_PATTERNS):
                dq, dk, dv = outs[4 + 3 * p:7 + 3 * p]
                o, lse = _banded_attention(dq, dk, dv, batch, seq, dil, window // dil,
                                           want_lse=True, out_dtype=F32)
                parts += [o, lse]
            yb = _merge_patterns(parts)
            w_out = w_out_odd[j]
        x1, top_idx, gates, hist = _outproj_norm_route(
            ya, yb, w_out, xf, ln1_g[layer], ln1_b[layer], router_w[layer], router_b[layer], alpha)
        table, block_e, nvalid = _routing_tables(top_idx, hist, n)
        y = _moe_experts(x1, table, block_e, nvalid, expert_w_gu, expert_b_gu, expert_w_dn,
                         expert_b_dn, layer, out_tiles=n * TOP_K + 2 * MOE_BLOCK)
        xf = _combine_norm(y, x1, gates, ln2_g[layer], ln2_b[layer], alpha, tm=COMBINE_TILE)
    return xf.reshape(batch, seq, d)
```
